```python
import math
import jax, jax.numpy as jnp
from jax import lax
import numpy as np

D_MODEL = 1024
BATCH = 2
SEQ = 16384
DEPTH = 1
DEC_BATCH = 4
DEC_SEQ = 8192
PAST_LEN = 128

GRID_W = 64
RW_HEADS = 8
RW_HEAD = 64
RW_WIDTH = RW_HEADS * RW_HEAD
DECAY_LORA = 64
ICL_LORA = 64
GATE_LORA = 128
DECAY_SCALE = math.exp(-0.5)
GN_EPS = 64e-5
ATT_HEADS = 8
ATT_KV_HEADS = 2
ATT_HEAD = 64
ATT_Q_WIDTH = ATT_HEADS * ATT_HEAD
ATT_KV_WIDTH = ATT_KV_HEADS * ATT_HEAD
Q_BLOCK = 128
ROPE_THETA = 10000.0
N_EXPERTS = 32
TOP_K = 4
D_FF = 1024
SWIGLU_LIMIT = 7.0
SWIGLU_ALPHA = 1.702
MOE_BLOCK = 256
EPS = 1e-6
N_MOD = 6

RW_SIZES = (RW_WIDTH, RW_WIDTH, RW_WIDTH, DECAY_LORA, DECAY_LORA, ICL_LORA, ICL_LORA, GATE_LORA)
RW_COLS = 3 * RW_WIDTH + 2 * DECAY_LORA + 2 * ICL_LORA + GATE_LORA
ATT_SIZES = (ATT_Q_WIDTH, ATT_KV_WIDTH, ATT_KV_WIDTH, 2 * D_MODEL)
IN_COLS = RW_COLS + ATT_Q_WIDTH + 2 * ATT_KV_WIDTH + 2 * D_MODEL

kernel_name = "hybrid_rwkv7_gqa_moe_encoder"


def _split(t, sizes):
    idx = np.cumsum(sizes)[:-1].tolist()
    return jnp.split(t, idx, axis=-1)


def _rmsnorm(x, g):
    xf = x.astype(jnp.float32)
    xf = xf * lax.rsqrt(jnp.mean(xf * xf, axis=-1, keepdims=True) + EPS)
    return xf * g


def _centred_shift(p, mu_prev, mu_next):
    prev = jnp.pad(p[:, :-1], ((0, 0), (1, 0), (0, 0)))
    nxt = jnp.pad(p[:, 1:], ((0, 0), (0, 1), (0, 0)))
    return p + mu_prev * (prev - p) + mu_next * (nxt - p)


def _wkv7_scan(r, w, k, v, kk, a, reverse):
    bsz, _, nh, hd = r.shape

    def step(S, inp):
        r_t, w_t, k_t, v_t, kk_t, a_t = inp
        s_kk = jnp.einsum('bhvk,bhk->bhv', S, kk_t)
        S = (S * w_t[:, :, None, :]
             - s_kk[..., None] * (a_t * kk_t)[:, :, None, :]
             + v_t[..., None] * k_t[:, :, None, :])
        return S, jnp.einsum('bhvk,bhk->bhv', S, r_t)

    xs = tuple(jnp.moveaxis(t, 1, 0) for t in (r, w, k, v, kk, a))
    s0 = jnp.zeros((bsz, nh, hd, hd), jnp.float32)
    _, ys = lax.scan(step, s0, xs, reverse=reverse)
    return jnp.moveaxis(ys, 0, 1)


def _axial_rope(n_tok):
    rows = n_tok // GRID_W
    row = jnp.repeat(jnp.arange(rows), GRID_W).astype(jnp.float32)
    col = jnp.tile(jnp.arange(GRID_W), rows).astype(jnp.float32)
    axis_dim = ATT_HEAD // 2
    freqs = ROPE_THETA ** (-jnp.arange(axis_dim // 2, dtype=jnp.float32) / (axis_dim // 2))
    ang = jnp.concatenate([row[:, None] * freqs, col[:, None] * freqs], axis=-1)
    return jnp.cos(ang), jnp.sin(ang)


def _apply_rope(x, cos, sin):
    x1 = x[..., 0::2]
    x2 = x[..., 1::2]
    c = cos[None, :, None, :]
    s = sin[None, :, None, :]
    return jnp.stack([x1 * c - x2 * s, x1 * s + x2 * c], axis=-1).reshape(x.shape)


def _block_attention(q, k, v):
    bsz, n_tok = q.shape[0], q.shape[1]
    grp = ATT_HEADS // ATT_KV_HEADS
    nb = n_tok // Q_BLOCK
    qb = q.reshape(bsz, nb, Q_BLOCK, ATT_KV_HEADS, grp, ATT_HEAD).transpose(1, 0, 2, 3, 4, 5)
    scale = ATT_HEAD ** -0.5

    def one_block(q_blk):
        s = jnp.einsum('bqhgd,bkhd->bhgqk', q_blk, k).astype(jnp.float32) * scale
        p = jax.nn.softmax(s, axis=-1)
        return jnp.einsum('bhgqk,bkhd->bqhgd', p, v)

    o = lax.map(one_block, qb)
    return o.transpose(1, 0, 2, 3, 4, 5).reshape(bsz, n_tok, ATT_Q_WIDTH)


def _moe(h, w_router, b_router, w_gu, b_gu, w_down, b_down):
    bsz, n_tok, dm = h.shape
    n = bsz * n_tok
    xt = h.reshape(n, dm)
    logits = (xt @ w_router + b_router).astype(jnp.float32)
    top_val, top_idx = lax.top_k(logits, TOP_K)
    gates = jax.nn.softmax(top_val, axis=-1)
    nk = n * TOP_K
    exp_flat = top_idx.reshape(nk)
    tok_flat = jnp.repeat(jnp.arange(n, dtype=jnp.int32), TOP_K)
    gate_flat = gates.reshape(nk)
    order = jnp.argsort(exp_flat)
    s_exp = exp_flat[order]
    s_tok = tok_flat[order]
    s_gate = gate_flat[order]
    counts = jnp.bincount(exp_flat, length=N_EXPERTS)
    padded = (counts + MOE_BLOCK - 1) // MOE_BLOCK * MOE_BLOCK
    pad_end = jnp.cumsum(padded)
    pad_start = pad_end - padded
    grp_start = jnp.cumsum(counts) - counts
    dest = pad_start[s_exp] + jnp.arange(nk) - grp_start[s_exp]
    cap = (-(-nk // MOE_BLOCK) + N_EXPERTS) * MOE_BLOCK
    nb = cap // MOE_BLOCK
    buf_tok = jnp.full((cap,), n, jnp.int32).at[dest].set(s_tok)
    buf_gate = jnp.zeros((cap,), jnp.float32).at[dest].set(s_gate)
    blk_exp = jnp.minimum(
        jnp.searchsorted(pad_end, jnp.arange(nb) * MOE_BLOCK, side='right'), N_EXPERTS - 1)
    x_pad = jnp.concatenate([xt, jnp.zeros((1, dm), xt.dtype)], axis=0)

    def expert_block(args):
        tok, gate, e = args
        xb = x_pad[tok]
        gu = xb @ w_gu[e] + b_gu[e]
        g_lin, u_lin = jnp.split(gu, 2, axis=-1)
        g_lin = jnp.minimum(g_lin, SWIGLU_LIMIT)
        u_lin = jnp.clip(u_lin, -SWIGLU_LIMIT, SWIGLU_LIMIT)
        act = (u_lin + 1.0) * (g_lin * jax.nn.sigmoid(SWIGLU_ALPHA * g_lin))
        out = act @ w_down[e] + b_down[e]
        return out * gate[:, None]

    outs = lax.map(expert_block, (buf_tok.reshape(nb, MOE_BLOCK), buf_gate.reshape(nb, MOE_BLOCK), blk_exp))
    y = jnp.zeros((n + 1, dm), outs.dtype).at[buf_tok].add(outs.reshape(cap, dm))
    return y[:n].reshape(bsz, n_tok, dm)


def _layer(x, c, lp):
    bsz, n_tok, _ = x.shape
    f32 = jnp.float32
    mod = (jax.nn.silu(c) @ lp['w_ada'] + lp['b_ada'])[:, None, :]
    sh1, sc1, gt1, sh2, sc2, gt2 = jnp.split(mod, N_MOD, axis=-1)

    h = _rmsnorm(x, lp['norm1_g']) * (1.0 + sc1) + sh1
    p = h @ lp['w_in']
    p_rw = _centred_shift(p[..., :RW_COLS], lp['mu_prev'], lp['mu_next'])
    r, k, v, dlo_f, dlo_b, alo_f, alo_b, glo = _split(p_rw, RW_SIZES)
    q, k_att, v_att, mgate = _split(p[..., RW_COLS:], ATT_SIZES)

    def heads(t):
        return t.reshape(bsz, n_tok, RW_HEADS, RW_HEAD).astype(f32)

    rh = heads(r)
    vh = heads(v)
    kk = heads(k * lp['k_k'])
    kk = kk * lax.rsqrt(jnp.maximum(jnp.sum(kk * kk, axis=-1, keepdims=True), 1e-24))
    o_rw = jnp.zeros((bsz, n_tok, RW_HEADS, RW_HEAD), f32)
    k_sum = jnp.zeros_like(o_rw)
    for dlo, alo, w0, wb, a0, ab, rev in ((dlo_f, alo_f, lp['w0_f'], lp['wb_f'], lp['a0_f'], lp['ab_f'], False),
                                          (dlo_b, alo_b, lp['w0_b'], lp['wb_b'], lp['a0_b'], lp['ab_b'], True)):
        w_dec = jnp.exp(-DECAY_SCALE * jax.nn.sigmoid(w0 + jnp.tanh(dlo) @ wb))
        a_icl = jax.nn.sigmoid(a0 + alo @ ab)
        k_dir = heads(k * (1.0 + (a_icl - 1.0) * lp['k_a']))
        o_rw = o_rw + _wkv7_scan(rh, heads(w_dec), k_dir, vh, kk, heads(a_icl), rev)
        k_sum = k_sum + k_dir
    mu = jnp.mean(o_rw, axis=-1, keepdims=True)
    var = jnp.mean(jnp.square(o_rw - mu), axis=-1, keepdims=True)
    o_rw = ((o_rw - mu) * lax.rsqrt(var + GN_EPS) * lp['lnx_w'].reshape(RW_HEADS, RW_HEAD)
            + lp['lnx_b'].reshape(RW_HEADS, RW_HEAD))
    o_rw = o_rw + jnp.sum(rh * k_sum * lp['r_k'], axis=-1, keepdims=True) * vh
    g_out = jax.nn.sigmoid(glo) @ lp['g_up']
    o_rw = o_rw.reshape(bsz, n_tok, RW_WIDTH) * g_out

    cos, sin = _axial_rope(n_tok)
    qh = _apply_rope(_rmsnorm(q.reshape(bsz, n_tok, ATT_HEADS, ATT_HEAD), lp['q_norm_g']), cos, sin)
    kh = _apply_rope(_rmsnorm(k_att.reshape(bsz, n_tok, ATT_KV_HEADS, ATT_HEAD), lp['k_norm_g']), cos, sin)
    vah = v_att.reshape(bsz, n_tok, ATT_KV_HEADS, ATT_HEAD).astype(f32)
    o_at = _block_attention(qh, kh, vah)

    gate_rw, gate_at = jnp.split(jax.nn.sigmoid(mgate.astype(f32)), 2, axis=-1)
    merged = gate_rw * (o_rw @ lp['w_br_rwkv']) + gate_at * (o_at @ lp['w_br_attn'])
    x = x + gt1 * (merged @ lp['w_out'])

    h2 = _rmsnorm(x, lp['norm2_g']) * (1.0 + sc2) + sh2
    x = x + gt2 * _moe(h2, lp['w_router'], lp['b_router'], lp['w_gu'], lp['b_gu'], lp['w_down'], lp['b_down'])
    return x


def setup_inputs(seed: int = 0) -> dict:
    key = jax.random.key(seed)
    keys = iter(jax.random.split(key, 48))
    f32 = jnp.float32

    def nrm(shape, scale):
        return jax.random.normal(next(keys), shape, f32) * scale

    def gain(shape):
        return 1.0 + nrm(shape, 0.05)

    L, D, E = DEPTH, D_MODEL, N_EXPERTS
    return {
        "x_prompt": nrm((BATCH, SEQ, D), 1.0),
        "x_sample": nrm((DEC_BATCH, DEC_SEQ, D), 1.0),
        "c_prompt": nrm((BATCH, D), 1.0),
        "c_sample": nrm((DEC_BATCH, D), 1.0),
        "norm1_g": gain((L, D)),
        "norm2_g": gain((L, D)),
        "w_ada": nrm((L, D, N_MOD * D), 0.5 * D ** -0.5),
        "b_ada": nrm((L, N_MOD * D), 0.02),
        "w_in": nrm((L, D, IN_COLS), D ** -0.5),
        "mu_prev": jax.random.uniform(next(keys), (L, RW_COLS), f32, 0.0, 0.5),
        "mu_next": jax.random.uniform(next(keys), (L, RW_COLS), f32, 0.0, 0.5),
        "w0_f": -0.5 + nrm((L, RW_WIDTH), 0.5),
        "w0_b": -0.5 + nrm((L, RW_WIDTH), 0.5),
        "wb_f": nrm((L, DECAY_LORA, RW_WIDTH), 0.5 * DECAY_LORA ** -0.5),
        "wb_b": nrm((L, DECAY_LORA, RW_WIDTH), 0.5 * DECAY_LORA ** -0.5),
        "a0_f": nrm((L, RW_WIDTH), 0.5),
        "a0_b": nrm((L, RW_WIDTH), 0.5),
        "ab_f": nrm((L, ICL_LORA, RW_WIDTH), 0.5 * ICL_LORA ** -0.5),
        "ab_b": nrm((L, ICL_LORA, RW_WIDTH), 0.5 * ICL_LORA ** -0.5),
        "k_k": 0.85 + nrm((L, RW_WIDTH), 0.1),
        "k_a": 1.0 + nrm((L, RW_WIDTH), 0.1),
        "r_k": nrm((L, RW_HEADS, RW_HEAD), 0.1),
        "g_up": nrm((L, GATE_LORA, RW_WIDTH), GATE_LORA ** -0.5),
        "lnx_w": gain((L, RW_WIDTH)),
        "lnx_b": nrm((L, RW_WIDTH), 0.02),
        "q_norm_g": gain((L, ATT_HEAD)),
        "k_norm_g": gain((L, ATT_HEAD)),
        "w_br_rwkv": nrm((L, RW_WIDTH, D), RW_WIDTH ** -0.5),
        "w_br_attn": nrm((L, ATT_Q_WIDTH, D), ATT_Q_WIDTH ** -0.5),
        "w_out": nrm((L, D, D), D ** -0.5),
        "w_router": nrm((L, D, E), D ** -0.5),
        "b_router": nrm((L, E), 0.01),
        "w_gu": nrm((L, E, D, 2 * D_FF), D ** -0.5),
        "b_gu": nrm((L, E, 2 * D_FF), 0.02),
        "w_down": nrm((L, E, D_FF, D), D_FF ** -0.5),
        "b_down": nrm((L, E, D), 0.02),
        "normf_g": gain((D,)),
    }


def reference(x_prompt, x_sample, c_prompt, c_sample, norm1_g, norm2_g, w_ada, b_ada, w_in, mu_prev, mu_next,
              w0_f, w0_b, wb_f, wb_b, a0_f, a0_b, ab_f, ab_b, k_k, k_a, r_k, g_up, lnx_w, lnx_b,
              q_norm_g, k_norm_g, w_br_rwkv, w_br_attn, w_out, w_router, b_router, w_gu, b_gu, w_down, b_down,
              normf_g):
    def layer_params(l):
        return dict(norm1_g=norm1_g[l], norm2_g=norm2_g[l], w_ada=w_ada[l], b_ada=b_ada[l], w_in=w_in[l],
                    mu_prev=mu_prev[l], mu_next=mu_next[l], w0_f=w0_f[l], w0_b=w0_b[l], wb_f=wb_f[l],
                    wb_b=wb_b[l], a0_f=a0_f[l], a0_b=a0_b[l], ab_f=ab_f[l], ab_b=ab_b[l], k_k=k_k[l],
                    k_a=k_a[l], r_k=r_k[l], g_up=g_up[l], lnx_w=lnx_w[l], lnx_b=lnx_b[l],
                    q_norm_g=q_norm_g[l], k_norm_g=k_norm_g[l], w_br_rwkv=w_br_rwkv[l],
                    w_br_attn=w_br_attn[l], w_out=w_out[l], w_router=w_router[l], b_router=b_router[l],
                    w_gu=w_gu[l], b_gu=b_gu[l], w_down=w_down[l], b_down=b_down[l])

    def run(x, c):
        for l in range(DEPTH):
            x = _layer(x, c, layer_params(l))
        return _rmsnorm(x, normf_g)

    y_prompt = run(x_prompt, c_prompt)
    y_sample = run(x_sample, c_sample)
    return (y_prompt, y_sample)
```

```python
import functools
import math

import jax
import jax.numpy as jnp
from jax import lax
from jax.experimental import pallas as pl
from jax.experimental.pallas import tpu as pltpu

F32 = jnp.float32
BF16 = jnp.bfloat16

D = 1024
GRID_W = 64
NH = 8
HD = 64
RW = NH * HD
RW_COLS = 1920
Q_OFF, K_OFF, V_OFF, G_OFF, IN_COLS = 1920, 2432, 2560, 2688, 4736
DECAY_SCALE = math.exp(-0.5)
GN_EPS = 64e-5
EPS = 1e-6
ROPE_THETA = 10000.0
N_EXPERTS = 32
TOP_K = 4
D_FF = 1024
SWIGLU_LIMIT = 7.0
SWIGLU_ALPHA = 1.702

TM = 256
CHUNK = 64
WKV_BLK = 256
HG = 256
ATT_TK = 512
MOE_BM = 512
LANES = 128
NEG_BIG = -1e30


def _cp(sem, vmem_mb=48):
    return pltpu.CompilerParams(dimension_semantics=sem, vmem_limit_bytes=vmem_mb << 20)


def _split3(x):
    h = x.astype(BF16)
    r1 = x - h.astype(F32)
    m = r1.astype(BF16)
    lo = (r1 - m.astype(F32)).astype(BF16)
    return h, m, lo


def _dot(a, b):
    return jnp.dot(a, b, preferred_element_type=F32)


def _dot_nt(a, b):
    return lax.dot_general(a, b, (((1,), (1,)), ((), ())), preferred_element_type=F32)


def _ada_kernel(c_ref, w_ref, b_ref, o_ref):
    c = c_ref[...]
    s = c * jax.nn.sigmoid(c)
    sh, sm, sl = _split3(s)
    wh, wm, wl = _split3(w_ref[...])
    acc = _dot(sh, wh) + (_dot(sh, wm) + _dot(sm, wh)) + (_dot(sm, wm) + _dot(sh, wl) + _dot(sl, wh))
    o_ref[...] = acc + b_ref[...]


def _ada(c8, w_ada, b_ada):
    n_mod = w_ada.shape[1] // D
    return pl.pallas_call(
        _ada_kernel,
        grid=(n_mod,),
        in_specs=[pl.BlockSpec((8, D), lambda j: (0, 0)),
                  pl.BlockSpec((D, D), lambda j: (0, j)),
                  pl.BlockSpec((1, D), lambda j: (0, j))],
        out_specs=pl.BlockSpec((8, D), lambda j: (0, j)),
        out_shape=jax.ShapeDtypeStruct((8, n_mod * D), F32),
        compiler_params=_cp(("arbitrary",)),
        name="ada",
    )(c8, w_ada, b_ada.reshape(1, -1))


def _rope(x, c, s, even):
    swap = jnp.where(even, pltpu.roll(x, LANES - 1, axis=1), pltpu.roll(x, 1, axis=1))
    return x * c + swap * s


def _inproj_kernel(x_ref, sc_ref, sh_ref, g_ref, w_ref, cos_ref, sin_ref, qg_ref, kg_ref, bd_ref,
                   prw_ref, q_ref, k_ref, v_ref, gate_ref):
    x = x_ref[...]
    ms = jnp.mean(x * x, axis=-1, keepdims=True)
    h = x * lax.rsqrt(ms + EPS) * g_ref[...]
    h = h * (1.0 + sc_ref[...]) + sh_ref[...]
    hb = h.astype(BF16)
    prw_ref[...] = _dot(hb, w_ref[:, 0:RW_COLS])
    v_ref[...] = _dot(hb, w_ref[:, V_OFF:G_OFF]).astype(BF16)
    gate_ref[...] = jax.nn.sigmoid(_dot(hb, w_ref[:, G_OFF:IN_COLS])).astype(BF16)

    cos = cos_ref[...]
    sin = sin_ref[...]
    bd = bd_ref[...]
    lane = lax.broadcasted_iota(jnp.int32, (TM, LANES), 1)
    even = (lane & 1) == 0
    low = lane < HD

    def norm_rope(slab, gain):
        ssq = _dot((slab * slab).astype(BF16), bd)
        return _rope(slab * lax.rsqrt(ssq * (1.0 / HD) + EPS) * gain, cos, sin, even)

    kf = _dot(hb, w_ref[:, K_OFF:V_OFF])
    k_ref[...] = norm_rope(kf, kg_ref[...]).astype(BF16)

    qf = _dot(hb, w_ref[:, Q_OFF:K_OFF])
    scale = HD ** -0.5
    for j in range(NH // 2):
        slab = norm_rope(qf[:, j * LANES:(j + 1) * LANES], qg_ref[...]) * scale
        rolled = pltpu.roll(slab, HD, axis=1)
        g = (2 * j) // 4
        if g == 0:
            qa = jnp.where(low, slab, 0.0)
            qb = jnp.where(low, rolled, 0.0)
        else:
            qa = jnp.where(low, 0.0, rolled)
            qb = jnp.where(low, 0.0, slab)
        ha = (2 * j) % 4
        q_ref[g, ha * TM:(ha + 1) * TM, :] = qa.astype(BF16)
        q_ref[g, (ha + 1) * TM:(ha + 2) * TM, :] = qb.astype(BF16)


def _inproj(x2, sc1, sh1, g1, w_in_b, cos_t, sin_t, qg, kg, bd128, n_tok):
    n = x2.shape[0]
    tpb = n_tok // TM
    row = lambda i: (i, 0)
    per_b = lambda i: (i // tpb, 0, 0)
    const2 = lambda i: (0, 0)
    tab = lambda i: (i % tpb, 0)
    return pl.pallas_call(
        _inproj_kernel,
        grid=(n // TM,),
        in_specs=[pl.BlockSpec((TM, D), row),
                  pl.BlockSpec((None, 1, D), per_b),
                  pl.BlockSpec((None, 1, D), per_b),
                  pl.BlockSpec((1, D), const2),
                  pl.BlockSpec((D, IN_COLS), const2),
                  pl.BlockSpec((TM, LANES), tab),
                  pl.BlockSpec((TM, LANES), tab),
                  pl.BlockSpec((1, LANES), const2),
                  pl.BlockSpec((1, LANES), const2),
                  pl.BlockSpec((LANES, LANES), const2)],
        out_specs=[pl.BlockSpec((TM, RW_COLS), row),
                   pl.BlockSpec((None, 2, 4 * TM, LANES), lambda i: (i, 0, 0, 0)),
                   pl.BlockSpec((TM, LANES), row),
                   pl.BlockSpec((TM, LANES), row),
                   pl.BlockSpec((TM, 2 * D), row)],
        out_shape=[jax.ShapeDtypeStruct((n, RW_COLS), F32),
                   jax.ShapeDtypeStruct((n // TM, 2, 4 * TM, LANES), BF16),
                   jax.ShapeDtypeStruct((n, LANES), BF16),
                   jax.ShapeDtypeStruct((n, LANES), BF16),
                   jax.ShapeDtypeStruct((n, 2 * D), BF16)],
        compiler_params=_cp(("arbitrary",), 56),
        name="inproj",
    )(x2, sc1, sh1, g1, w_in_b, cos_t, sin_t, qg, kg, bd128)


def _prep_kernel(p_ref, pprev_ref, pnext_ref, mup_ref, mun_ref, wdec_ref, wicl_ref, gup_ref,
                 w0_ref, a0_ref, kk_ref, ka_ref, rk_ref, bd_ref,
                 r_o, v_o, kk_o, lwf_o, lwb_o, kf_o, kb_o, bf_o, bb_o, go_o, bv_o, *, tpb):
    i = pl.program_id(0)
    ib = i % tpb
    p = p_ref[...]
    row = lax.broadcasted_iota(jnp.int32, (TM, 1), 0)
    hp = jnp.where(ib == 0, 0.0, pprev_ref[7:8, :])
    hn = jnp.where(ib == tpb - 1, 0.0, pnext_ref[0:1, :])
    prev = jnp.where(row == 0, hp, pltpu.roll(p, 1, axis=0))
    nxt = jnp.where(row == TM - 1, hn, pltpu.roll(p, TM - 1, axis=0))
    ps = p + mup_ref[...] * (prev - p) + mun_ref[...] * (nxt - p)

    r = ps[:, 0:RW]
    k = ps[:, RW:2 * RW]
    v = ps[:, 2 * RW:3 * RW]
    dlo = ps[:, 3 * RW:3 * RW + LANES]
    alo = ps[:, 3 * RW + LANES:3 * RW + 2 * LANES]
    glo = ps[:, 3 * RW + 2 * LANES:3 * RW + 3 * LANES]
    bd = bd_ref[...]

    dec = _dot(jnp.tanh(dlo).astype(BF16), wdec_ref[...]) + w0_ref[...]
    icl = _dot(alo.astype(BF16), wicl_ref[...]) + a0_ref[...]
    lw = -DECAY_SCALE * jax.nn.sigmoid(dec)
    a = jax.nn.sigmoid(icl)
    g_out = _dot(jax.nn.sigmoid(glo).astype(BF16), gup_ref[...])

    kk = k * kk_ref[...]
    ssq = _dot((kk * kk).astype(BF16), bd)
    kk = kk * lax.rsqrt(jnp.maximum(ssq, 1e-24))
    ka = ka_ref[...]
    a_f = a[:, 0:RW]
    a_b = a[:, RW:2 * RW]
    k_f = k * (1.0 + (a_f - 1.0) * ka)
    k_b = k * (1.0 + (a_b - 1.0) * ka)
    bonus = _dot((r * (k_f + k_b) * rk_ref[...]).astype(BF16), bd)

    r_o[...] = r
    v_o[...] = v
    kk_o[...] = kk
    lwf_o[...] = lw[:, 0:RW]
    lwb_o[...] = lw[:, RW:2 * RW]
    kf_o[...] = k_f
    kb_o[...] = k_b
    bf_o[...] = a_f * kk
    bb_o[...] = a_b * kk
    go_o[...] = g_out
    bv_o[...] = bonus * v


def _prep(p_rw, mup, mun, wdec, wicl, gup, w0, a0, k_k, k_a, r_k, bd512, n_tok):
    n = p_rw.shape[0]
    tpb = n_tok // TM
    nt8 = n // 8
    row = lambda i: (i, 0)
    c2 = lambda i: (0, 0)
    out = jax.ShapeDtypeStruct((n, RW), F32)
    return pl.pallas_call(
        functools.partial(_prep_kernel, tpb=tpb),
        grid=(n // TM,),
        in_specs=[pl.BlockSpec((TM, RW_COLS), row),
                  pl.BlockSpec((8, RW_COLS), lambda i: (jnp.maximum(i * (TM // 8) - 1, 0), 0)),
                  pl.BlockSpec((8, RW_COLS), lambda i: (jnp.minimum((i + 1) * (TM // 8), nt8 - 1), 0)),
                  pl.BlockSpec((1, RW_COLS), c2), pl.BlockSpec((1, RW_COLS), c2),
                  pl.BlockSpec((LANES, 2 * RW), c2), pl.BlockSpec((LANES, 2 * RW), c2),
                  pl.BlockSpec((LANES, RW), c2),
                  pl.BlockSpec((1, 2 * RW), c2), pl.BlockSpec((1, 2 * RW), c2),
                  pl.BlockSpec((1, RW), c2), pl.BlockSpec((1, RW), c2), pl.BlockSpec((1, RW), c2),
                  pl.BlockSpec((RW, RW), c2)],
        out_specs=[pl.BlockSpec((TM, RW), row)] * 11,
        out_shape=[out] * 11,
        compiler_params=_cp(("arbitrary",)),
        name="rwkv_prep",
    )(p_rw, p_rw, p_rw, mup, mun, wdec, wicl, gup, w0, a0, k_k, k_a, r_k, bd512)


def _bd(x, bdmask):
    return jnp.where(bdmask, jnp.concatenate([x, x, x, x], axis=0), 0.0)


def _wkv_chunk(sl, r_ref, v_ref, kk_ref, lw_ref, k_ref, b_ref, tri_ref, st_ref, o_ref, rev, cst):
    strict, incl, eye_w, bdmask = cst
    r = r_ref[sl, :]
    v = v_ref[sl, :]
    kk = kk_ref[sl, :]
    lw = lw_ref[sl, :]
    k = k_ref[sl, :]
    b = b_ref[sl, :]
    tri = tri_ref[...]
    l3 = _split3(lw)
    cs = _dot(tri, l3[0]) + _dot(tri, l3[1]) + _dot(tri, l3[2])
    tot = cs[0:1, :] if rev else cs[CHUNK - 1:CHUNK, :]
    e_pos = jnp.exp(cs)
    e_prev = jnp.exp(cs - lw)
    e_neg = jnp.exp(-cs)
    e_end = jnp.exp(tot - cs)
    rq = r * e_pos
    kq = kk * e_prev
    bdn = b * e_neg
    kdn = k * e_neg
    bend = b * e_end
    kend = k * e_end

    lhs1 = jnp.concatenate([kq, rq], axis=0).astype(BF16)
    rhs1 = jnp.concatenate([_bd(bdn, bdmask), _bd(kdn, bdmask)], axis=0).astype(BF16)
    a_all = _dot_nt(lhs1, rhs1)
    n_ub = jnp.where(strict, a_all[0:CHUNK, 0:HG], 0.0)
    a_uk = jnp.where(strict, a_all[0:CHUNK, HG:2 * HG], 0.0)
    a_rb = jnp.where(incl, a_all[CHUNK:2 * CHUNK, 0:HG], 0.0)
    a_rk = jnp.where(incl, a_all[CHUNK:2 * CHUNK, HG:2 * HG], 0.0)

    t_inv = jnp.where(eye_w, 1.0, 0.0) - n_ub
    pw = n_ub
    for _ in range(int(math.log2(CHUNK)) - 1):
        pw = _dot(pw.astype(BF16), _bd(pw, bdmask).astype(BF16))
        t_inv = t_inv + _dot(t_inv.astype(BF16), _bd(pw, bdmask).astype(BF16))

    vb = _bd(v, bdmask).astype(BF16)
    x0 = _dot(a_uk.astype(BF16), vb)
    tb = t_inv.astype(BF16)
    u0 = -_dot(tb, _bd(x0, bdmask).astype(BF16))
    pm = _dot(tb, _bd(kq, bdmask).astype(BF16))

    st = st_ref[...]
    g = _dot(jnp.concatenate([pm, rq], axis=0).astype(BF16), st.astype(BF16))
    u = u0 - g[0:CHUNK]
    y = g[CHUNK:2 * CHUNK] + _dot(jnp.concatenate([a_rb, a_rk], axis=1).astype(BF16),
                                  jnp.concatenate([_bd(u, bdmask).astype(BF16), vb], axis=0))
    o_ref[sl, :] = y

    lhs_t = jnp.concatenate([bend, kend], axis=0).T.astype(BF16)
    upd = _dot(lhs_t, jnp.concatenate([u, v], axis=0).astype(BF16))
    wcol = jnp.exp(jnp.broadcast_to(tot, (LANES, HG)).T)
    st_ref[...] = st * jnp.concatenate([wcol, wcol], axis=1) + jnp.where(bdmask, upd, 0.0)


def _wkv_kernel(rf, vf, kkf, lwf, kf, bf, rb, vb, kkb, lwb, kb, bb, trif, trib, of, ob, stf, stb):
    @pl.when(pl.program_id(2) == 0)
    def _():
        stf[...] = jnp.zeros_like(stf)
        stb[...] = jnp.zeros_like(stb)

    row = lax.broadcasted_iota(jnp.int32, (CHUNK, HG), 0)
    s_idx = lax.broadcasted_iota(jnp.int32, (CHUNK, HG), 1) & (HD - 1)
    r2 = lax.broadcasted_iota(jnp.int32, (HG, HG), 0) >> 6
    c2 = lax.broadcasted_iota(jnp.int32, (HG, HG), 1) >> 6
    bdmask = r2 == c2
    eye_w = s_idx == row
    cst_f = (s_idx < row, s_idx <= row, eye_w, bdmask)
    cst_b = (s_idx > row, s_idx >= row, eye_w, bdmask)
    nc = WKV_BLK // CHUNK

    def body(c, carry):
        sf = pl.ds(pl.multiple_of(c * CHUNK, CHUNK), CHUNK)
        sb = pl.ds(pl.multiple_of((nc - 1 - c) * CHUNK, CHUNK), CHUNK)
        _wkv_chunk(sf, rf, vf, kkf, lwf, kf, bf, trif, stf, of, False, cst_f)
        _wkv_chunk(sb, rb, vb, kkb, lwb, kb, bb, trib, stb, ob, True, cst_b)
        return carry

    lax.fori_loop(0, nc, body, 0)


def _wkv(r, v, kk, lwf, lwb, kf, kb, bf, bb, trif, trib, bsz, n_tok):
    nb = n_tok // WKV_BLK
    sh3 = lambda a: a.reshape(bsz, n_tok, RW)
    fwd = pl.BlockSpec((None, WKV_BLK, HG), lambda b, g, i: (b, i, g))
    bwd = pl.BlockSpec((None, WKV_BLK, HG), lambda b, g, i: (b, nb - 1 - i, g))
    tri = pl.BlockSpec((CHUNK, CHUNK), lambda b, g, i: (0, 0))
    out = jax.ShapeDtypeStruct((bsz, n_tok, RW), F32)
    of, ob = pl.pallas_call(
        _wkv_kernel,
        grid=(bsz, RW // HG, nb),
        in_specs=[fwd] * 6 + [bwd] * 6 + [tri, tri],
        out_specs=[fwd, bwd],
        out_shape=[out, out],
        scratch_shapes=[pltpu.VMEM((HG, HG), F32), pltpu.VMEM((HG, HG), F32)],
        compiler_params=_cp(("arbitrary", "arbitrary", "arbitrary")),
        name="wkv",
    )(sh3(r), sh3(v), sh3(kk), sh3(lwf), sh3(kf), sh3(bf),
      sh3(r), sh3(v), sh3(kk), sh3(lwb), sh3(kb), sh3(bb), trif, trib)
    return of.reshape(-1, RW), ob.reshape(-1, RW)


def _attn_kernel(q_ref, k_ref, v_ref, o_ref, m_scr, l_scr, acc_scr, *, n_kb, tk):
    g = pl.program_id(2)
    q = q_ref[...]
    m_scr[...] = jnp.full_like(m_scr, NEG_BIG)
    l_scr[...] = jnp.zeros_like(l_scr)
    acc_scr[...] = jnp.zeros_like(acc_scr)

    def body(kb, carry):
        sl = pl.ds(pl.multiple_of(kb * tk, tk), tk)
        s = _dot_nt(q, k_ref[sl, :])
        m_prev = m_scr[...]
        m_new = jnp.maximum(m_prev, jnp.max(s, axis=1, keepdims=True))
        alpha = jnp.exp(m_prev - m_new)
        p = jnp.exp(s - m_new[:, 0:1])
        l_scr[...] = alpha * l_scr[...] + jnp.sum(p, axis=1, keepdims=True)
        acc_scr[...] = alpha * acc_scr[...] + _dot(p.astype(BF16), v_ref[sl, :])
        m_scr[...] = m_new
        return carry

    lax.fori_loop(0, n_kb, body, 0)
    lane = lax.broadcasted_iota(jnp.int32, (4 * TM, LANES), 1)
    valid = (lane >= g * HD) & (lane < (g + 1) * HD)
    o = jnp.where(valid, acc_scr[...] / l_scr[...], 0.0).astype(BF16)
    for hh in range(4):
        o_ref[:, hh * LANES:(hh + 1) * LANES] = o[hh * TM:(hh + 1) * TM, :]


def _attention(q_st, k, v, bsz, n_tok):
    nq = n_tok // TM
    tk = min(ATT_TK, n_tok)
    q5 = q_st.reshape(bsz, nq, 2, 4 * TM, LANES)
    k3 = k.reshape(bsz, n_tok, LANES)
    v3 = v.reshape(bsz, n_tok, LANES)
    kv = pl.BlockSpec((None, n_tok, LANES), lambda b, i, g: (b, 0, 0))
    o = pl.pallas_call(
        functools.partial(_attn_kernel, n_kb=n_tok // tk, tk=tk),
        grid=(bsz, nq, 2),
        in_specs=[pl.BlockSpec((None, None, None, 4 * TM, LANES), lambda b, i, g: (b, i, g, 0, 0)), kv, kv],
        out_specs=pl.BlockSpec((None, TM, 4 * LANES), lambda b, i, g: (b, i, g)),
        out_shape=jax.ShapeDtypeStruct((bsz, n_tok, NH * LANES), BF16),
        scratch_shapes=[pltpu.VMEM((4 * TM, LANES), F32)] * 3,
        compiler_params=_cp(("arbitrary", "arbitrary", "arbitrary"), 56),
        name="attention",
    )(q5, k3, v3)
    return o.reshape(-1, NH * LANES)


def _merge_kernel(of_ref, ob_ref, bv_ref, go_ref, oat_ref, gate_ref, x_ref, gt1_ref, sc2_ref, sh2_ref,
                  lnw_ref, lnb_ref, g2_ref, wbr_ref, wba_ref, wout_ref, wrh_ref, wrl_ref, br_ref,
                  bd_ref, tri_ref, x1_ref, h2_ref, route_ref, cnt_ref, carry):
    @pl.when(pl.program_id(0) == 0)
    def _():
        carry[...] = jnp.zeros_like(carry)

    bd = bd_ref[...]
    o = of_ref[...] + ob_ref[...]
    oh, om, _ = _split3(o)
    mu = (_dot(oh, bd) + _dot(om, bd)) * (1.0 / HD)
    d = o - mu
    dh, dm, _ = _split3(d * d)
    var = (_dot(dh, bd) + _dot(dm, bd)) * (1.0 / HD)
    on = d * lax.rsqrt(var + GN_EPS) * lnw_ref[...] + lnb_ref[...]
    o_rw = (on + bv_ref[...]) * go_ref[...]
    br = _dot(o_rw.astype(BF16), wbr_ref[...])
    ba = _dot(oat_ref[...], wba_ref[...])
    gates = gate_ref[...]
    merged = gates[:, 0:D].astype(F32) * br + gates[:, D:2 * D].astype(F32) * ba
    x1 = x_ref[...] + gt1_ref[...] * _dot(merged.astype(BF16), wout_ref[...])
    x1_ref[...] = x1
    ms = jnp.mean(x1 * x1, axis=-1, keepdims=True)
    h2 = x1 * lax.rsqrt(ms + EPS) * g2_ref[...]
    h2 = h2 * (1.0 + sc2_ref[...]) + sh2_ref[...]
    h2_ref[...] = h2

    hh, hm, hl = _split3(h2)
    wh = wrh_ref[...]
    wl = wrl_ref[...]
    logits = _dot(hh, wh) + (_dot(hh, wl) + _dot(hm, wh)) + (_dot(hm, wl) + _dot(hl, wh)) + br_ref[...]

    lane = lax.broadcasted_iota(jnp.int32, (TM, LANES), 1).astype(F32)
    cur = logits
    vals, idxs = [], []
    for _ in range(TOP_K):
        m = jnp.max(cur, axis=1, keepdims=True)
        ix = jnp.min(jnp.where(cur == m, lane, float(LANES)), axis=1, keepdims=True)
        vals.append(m)
        idxs.append(ix)
        cur = jnp.where(lane == ix, -jnp.inf, cur)
    es = [jnp.exp(vv - vals[0]) for vv in vals]
    den = es[0] + es[1] + es[2] + es[3]
    onehot = jnp.zeros((TM, LANES), F32)
    for ix in idxs:
        onehot = onehot + jnp.where(lane == ix, 1.0, 0.0)
    cnt = _dot(tri_ref[...], onehot.astype(BF16)) + carry[...]
    route = jnp.zeros((TM, LANES), F32)
    for j in range(TOP_K):
        rank = jnp.sum(jnp.where(lane == idxs[j], cnt, 0.0), axis=1, keepdims=True)
        route = jnp.where(lane == float(j), idxs[j], route)
        route = jnp.where(lane == float(TOP_K + j), rank, route)
        route = jnp.where(lane == float(2 * TOP_K + j), es[j] / den, route)
    route_ref[...] = route
    carry[...] = carry[...] + jnp.sum(onehot, axis=0, keepdims=True)
    cnt_ref[...] = jnp.broadcast_to(carry[...], (8, LANES))


def _merge(of, ob, bv, go, oat, gates, x2, gt1, sc2, sh2, lnw, lnb, g2, wbr, wba, wout, wrh, wrl, brt,
           bd512, tri_tm, n_tok):
    n = x2.shape[0]
    tpb = n_tok // TM
    row = lambda i: (i, 0)
    per_b = lambda i: (i // tpb, 0, 0)
    c2 = lambda i: (0, 0)
    rw = pl.BlockSpec((TM, RW), row)
    full = lambda a: pl.BlockSpec(a.shape, c2)
    mod = pl.BlockSpec((None, 1, D), per_b)
    return pl.pallas_call(
        _merge_kernel,
        grid=(n // TM,),
        in_specs=[rw, rw, rw, rw, pl.BlockSpec((TM, NH * LANES), row), pl.BlockSpec((TM, 2 * D), row),
                  pl.BlockSpec((TM, D), row), mod, mod, mod,
                  full(lnw), full(lnb), full(g2), full(wbr), full(wba), full(wout), full(wrh), full(wrl),
                  full(brt), full(bd512), full(tri_tm)],
        out_specs=[pl.BlockSpec((TM, D), row), pl.BlockSpec((TM, D), row), pl.BlockSpec((TM, LANES), row),
                   pl.BlockSpec((8, LANES), c2)],
        out_shape=[jax.ShapeDtypeStruct((n, D), F32), jax.ShapeDtypeStruct((n, D), F32),
                   jax.ShapeDtypeStruct((n, LANES), F32), jax.ShapeDtypeStruct((8, LANES), F32)],
        scratch_shapes=[pltpu.VMEM((1, LANES), F32)],
        compiler_params=_cp(("arbitrary",)),
        name="merge",
    )(of, ob, bv, go, oat, gates, x2, gt1, sc2, sh2, lnw, lnb, g2, wbr, wba, wout, wrh, wrl, brt,
      bd512, tri_tm)


def _dispatch_kernel(idx_ref, rank_ref, pstart_ref, h2_hbm, xs_in_hbm, xs_hbm, sem):
    del xs_in_hbm
    base = pl.program_id(0) * TM

    def body(t, carry):
        for j in range(TOP_K):
            dst = pstart_ref[idx_ref[t * TOP_K + j]] + rank_ref[t * TOP_K + j]
            pltpu.make_async_copy(h2_hbm.at[pl.ds(base + t, 1)], xs_hbm.at[pl.ds(dst, 1)], sem).start()
        return carry

    lax.fori_loop(0, TM, body, 0)
    pltpu.make_async_copy(h2_hbm.at[pl.ds(0, TM * TOP_K)], xs_hbm.at[pl.ds(0, TM * TOP_K)], sem).wait()


def _dispatch(idx_flat, rank_flat, pstart, h2, xs0):
    n = h2.shape[0]
    smem_blk = pl.BlockSpec((TM * TOP_K,), lambda i: (i,), memory_space=pltpu.SMEM)
    return pl.pallas_call(
        _dispatch_kernel,
        grid=(n // TM,),
        in_specs=[smem_blk, smem_blk,
                  pl.BlockSpec(memory_space=pltpu.SMEM),
                  pl.BlockSpec(memory_space=pl.ANY),
                  pl.BlockSpec(memory_space=pl.ANY)],
        out_specs=pl.BlockSpec(memory_space=pl.ANY),
        out_shape=jax.ShapeDtypeStruct(xs0.shape, xs0.dtype),
        scratch_shapes=[pltpu.SemaphoreType.DMA(())],
        input_output_aliases={4: 0},
        compiler_params=_cp(("arbitrary",)),
        name="moe_dispatch",
    )(idx_flat, rank_flat, pstart, h2, xs0)


def _expert_kernel(bexp_ref, nused_ref, xs_ref, wgu_ref, bgu_ref, wd_ref, bdn_ref, ys_ref):
    i = pl.program_id(0)

    @pl.when(i < nused_ref[0])
    def _():
        gu = _dot(xs_ref[...].astype(BF16), wgu_ref[...]) + bgu_ref[...]
        g_lin = jnp.minimum(gu[:, 0:D_FF], SWIGLU_LIMIT)
        u_lin = jnp.clip(gu[:, D_FF:2 * D_FF], -SWIGLU_LIMIT, SWIGLU_LIMIT)
        act = (u_lin + 1.0) * (g_lin * jax.nn.sigmoid(SWIGLU_ALPHA * g_lin))
        ys_ref[...] = _dot(act.astype(BF16), wd_ref[...]) + bdn_ref[...]

    @pl.when(i >= nused_ref[0])
    def _():
        ys_ref[...] = jnp.zeros_like(ys_ref)


def _experts(blk_exp, nused, xs, wgu_b, bgu, wd_b, bdn):
    cap = xs.shape[0]
    grid_spec = pltpu.PrefetchScalarGridSpec(
        num_scalar_prefetch=2,
        grid=(cap // MOE_BM,),
        in_specs=[pl.BlockSpec((MOE_BM, D), lambda i, be, nu: (i, 0)),
                  pl.BlockSpec((None, D, 2 * D_FF), lambda i, be, nu: (be[i], 0, 0)),
                  pl.BlockSpec((None, 1, 2 * D_FF), lambda i, be, nu: (be[i], 0, 0)),
                  pl.BlockSpec((None, D_FF, D), lambda i, be, nu: (be[i], 0, 0)),
                  pl.BlockSpec((None, 1, D), lambda i, be, nu: (be[i], 0, 0))],
        out_specs=pl.BlockSpec((MOE_BM, D), lambda i, be, nu: (i, 0)),
    )
    return pl.pallas_call(
        _expert_kernel,
        grid_spec=grid_spec,
        out_shape=jax.ShapeDtypeStruct((cap, D), F32),
        compiler_params=_cp(("arbitrary",), 56),
        name="moe_experts",
    )(blk_exp, nused, xs, wgu_b, bgu, wd_b, bdn)


def _combine_kernel(idx_ref, rank_ref, pstart_ref, ys_hbm, route_ref, x1_ref, gt2_ref, gf_ref, o_ref,
                    buf, sem):
    def body(t, carry):
        for j in range(TOP_K):
            src = pstart_ref[idx_ref[t * TOP_K + j]] + rank_ref[t * TOP_K + j]
            pltpu.make_async_copy(ys_hbm.at[pl.ds(src, 1)], buf.at[j, pl.ds(t, 1)], sem).start()
        return carry

    lax.fori_loop(0, TM, body, 0)
    for j in range(TOP_K):
        pltpu.make_async_copy(ys_hbm.at[pl.ds(0, TM)], buf.at[j], sem).wait()
    route = route_ref[...]
    y = jnp.zeros((TM, D), F32)
    for j in range(TOP_K):
        y = y + route[:, 2 * TOP_K + j:2 * TOP_K + j + 1] * buf[j]
    x = x1_ref[...] + gt2_ref[...] * y
    ms = jnp.mean(x * x, axis=-1, keepdims=True)
    o_ref[...] = x * lax.rsqrt(ms + EPS) * gf_ref[...]


def _combine(idx_flat, rank_flat, pstart, ys, route, x1, gt2, gf, n_tok):
    n = x1.shape[0]
    tpb = n_tok // TM
    row = lambda i: (i, 0)
    smem_blk = pl.BlockSpec((TM * TOP_K,), lambda i: (i,), memory_space=pltpu.SMEM)
    return pl.pallas_call(
        _combine_kernel,
        grid=(n // TM,),
        in_specs=[smem_blk, smem_blk,
                  pl.BlockSpec(memory_space=pltpu.SMEM),
                  pl.BlockSpec(memory_space=pl.ANY),
                  pl.BlockSpec((TM, LANES), row),
                  pl.BlockSpec((TM, D), row),
                  pl.BlockSpec((None, 1, D), lambda i: (i // tpb, 0, 0)),
                  pl.BlockSpec((1, D), lambda i: (0, 0))],
        out_specs=pl.BlockSpec((TM, D), row),
        out_shape=jax.ShapeDtypeStruct((n, D), F32),
        scratch_shapes=[pltpu.VMEM((TOP_K, TM, D), F32), pltpu.SemaphoreType.DMA(())],
        compiler_params=_cp(("arbitrary",)),
        name="moe_combine",
    )(idx_flat, rank_flat, pstart, ys, route, x1, gt2, gf)


def _rope_tables(n_tok):
    t = jnp.arange(n_tok)
    row = (t // GRID_W).astype(F32)
    col = (t % GRID_W).astype(F32)
    nf = HD // 4
    freqs = ROPE_THETA ** (-jnp.arange(nf, dtype=F32) / nf)
    ang = jnp.concatenate([row[:, None] * freqs, col[:, None] * freqs], axis=-1)
    cos = jnp.repeat(jnp.cos(ang), 2, axis=-1)
    sin = jnp.repeat(jnp.sin(ang), 2, axis=-1)
    sign = jnp.tile(jnp.array([-1.0, 1.0], F32), HD // 2)
    return jnp.tile(cos, (1, 2)), jnp.tile(sin * sign, (1, 2))


def _block_diag2(a, b):
    z = jnp.zeros_like(a)
    return jnp.concatenate([jnp.concatenate([a, z], axis=1), jnp.concatenate([z, b], axis=1)], axis=0)


def _prepare_weights(w):
    f = {}
    f["w_in"] = w["w_in"][0].astype(BF16)
    f["g1"] = w["norm1_g"][0].reshape(1, D)
    f["g2"] = w["norm2_g"][0].reshape(1, D)
    f["qg"] = jnp.tile(w["q_norm_g"][0], 2).reshape(1, LANES)
    f["kg"] = jnp.tile(w["k_norm_g"][0], 2).reshape(1, LANES)
    ones = jnp.ones((HD, HD), F32)
    f["bd128"] = jnp.kron(jnp.eye(2, dtype=F32), ones).astype(BF16)
    f["bd512"] = jnp.kron(jnp.eye(NH, dtype=F32), ones).astype(BF16)
    f["mup"] = w["mu_prev"][0].reshape(1, RW_COLS)
    f["mun"] = w["mu_next"][0].reshape(1, RW_COLS)
    f["wdec"] = _block_diag2(w["wb_f"][0], w["wb_b"][0]).astype(BF16)
    f["wicl"] = _block_diag2(w["ab_f"][0], w["ab_b"][0]).astype(BF16)
    f["gup"] = w["g_up"][0].astype(BF16)
    f["w0"] = jnp.concatenate([w["w0_f"][0], w["w0_b"][0]]).reshape(1, 2 * RW)
    f["a0"] = jnp.concatenate([w["a0_f"][0], w["a0_b"][0]]).reshape(1, 2 * RW)
    f["k_k"] = w["k_k"][0].reshape(1, RW)
    f["k_a"] = w["k_a"][0].reshape(1, RW)
    f["r_k"] = w["r_k"][0].reshape(1, RW)
    ti = jnp.arange(CHUNK)
    f["trif"] = (ti[None, :] <= ti[:, None]).astype(BF16)
    f["trib"] = (ti[None, :] >= ti[:, None]).astype(BF16)
    tm = jnp.arange(TM)
    f["tri_tm"] = (tm[None, :] < tm[:, None]).astype(BF16)
    f["lnw"] = w["lnx_w"][0].reshape(1, RW)
    f["lnb"] = w["lnx_b"][0].reshape(1, RW)
    f["wbr"] = w["w_br_rwkv"][0].astype(BF16)
    wa = w["w_br_attn"][0].reshape(NH, HD, D)
    z = jnp.zeros_like(wa)
    lowhalf = (jnp.arange(NH) < NH // 2)[:, None, None]
    wa_pad = jnp.concatenate([jnp.where(lowhalf, wa, z), jnp.where(lowhalf, z, wa)], axis=1)
    f["wba"] = wa_pad.reshape(NH * LANES, D).astype(BF16)
    f["wout"] = w["w_out"][0].astype(BF16)
    wr = jnp.pad(w["w_router"][0], ((0, 0), (0, LANES - N_EXPERTS)))
    wrh = wr.astype(BF16)
    f["wrh"] = wrh
    f["wrl"] = (wr - wrh.astype(F32)).astype(BF16)
    f["br"] = jnp.pad(w["b_router"][0], (0, LANES - N_EXPERTS), constant_values=NEG_BIG).reshape(1, LANES)
    f["wgu"] = w["w_gu"][0].astype(BF16)
    f["bgu"] = w["b_gu"][0].reshape(N_EXPERTS, 1, 2 * D_FF)
    f["wd"] = w["w_down"][0].astype(BF16)
    f["bdn"] = w["b_down"][0].reshape(N_EXPERTS, 1, D)
    f["gf"] = w["normf_g"].reshape(1, D)
    return f


def _run(x, c, w, f):
    bsz, n_tok, _ = x.shape
    n = bsz * n_tok
    x2 = x.reshape(n, D)
    c8 = jnp.pad(c, ((0, 8 - bsz), (0, 0)))
    mod = _ada(c8, w["w_ada"][0], w["b_ada"][0])[:bsz]
    sh1, sc1, gt1, sh2, sc2, gt2 = [m.reshape(bsz, 1, D) for m in jnp.split(mod, 6, axis=-1)]

    cos_t, sin_t = _rope_tables(n_tok)
    p_rw, q_st, k_att, v_att, gates = _inproj(x2, sc1, sh1, f["g1"], f["w_in"], cos_t, sin_t,
                                              f["qg"], f["kg"], f["bd128"], n_tok)
    r, v, kk, lwf, lwb, kf, kb, bf, bb, go, bv = _prep(
        p_rw, f["mup"], f["mun"], f["wdec"], f["wicl"], f["gup"], f["w0"], f["a0"],
        f["k_k"], f["k_a"], f["r_k"], f["bd512"], n_tok)
    of, ob = _wkv(r, v, kk, lwf, lwb, kf, kb, bf, bb, f["trif"], f["trib"], bsz, n_tok)
    o_at = _attention(q_st, k_att, v_att, bsz, n_tok)
    x1, h2, route, cnt = _merge(of, ob, bv, go, o_at, gates, x2, gt1, sc2, sh2, f["lnw"], f["lnb"], f["g2"],
                                f["wbr"], f["wba"], f["wout"], f["wrh"], f["wrl"], f["br"],
                                f["bd512"], f["tri_tm"], n_tok)

    counts = cnt[0, :N_EXPERTS].astype(jnp.int32)
    padded = (counts + MOE_BM - 1) // MOE_BM * MOE_BM
    pad_end = jnp.cumsum(padded)
    pstart = (pad_end - padded).astype(jnp.int32)
    nk = n * TOP_K
    nblk = -(-nk // MOE_BM) + N_EXPERTS
    blk_exp = jnp.minimum(jnp.searchsorted(pad_end, jnp.arange(nblk) * MOE_BM, side="right"),
                          N_EXPERTS - 1).astype(jnp.int32)
    nused = (pad_end[-1:] // MOE_BM).astype(jnp.int32)
    idx_flat = route[:, 0:TOP_K].astype(jnp.int32).reshape(nk)
    rank_flat = route[:, TOP_K:2 * TOP_K].astype(jnp.int32).reshape(nk)

    xs = _dispatch(idx_flat, rank_flat, pstart, h2, jnp.zeros((nblk * MOE_BM, D), F32))
    ys = _experts(blk_exp, nused, xs, f["wgu"], f["bgu"], f["wd"], f["bdn"])
    y = _combine(idx_flat, rank_flat, pstart, ys, route, x1, gt2, f["gf"], n_tok)
    return y.reshape(bsz, n_tok, D)


def kernel(x_prompt, x_sample, c_prompt, c_sample, norm1_g, norm2_g, w_ada, b_ada, w_in, mu_prev, mu_next, w0_f, w0_b, wb_f, wb_b, a0_f, a0_b, ab_f, ab_b, k_k, k_a, r_k, g_up, lnx_w, lnx_b, q_norm_g, k_norm_g, w_br_rwkv, w_br_attn, w_out, w_router, b_router, w_gu, b_gu, w_down, b_down, normf_g):
    w = dict(norm1_g=norm1_g, norm2_g=norm2_g, w_ada=w_ada, b_ada=b_ada, w_in=w_in, mu_prev=mu_prev,
             mu_next=mu_next, w0_f=w0_f, w0_b=w0_b, wb_f=wb_f, wb_b=wb_b, a0_f=a0_f, a0_b=a0_b, ab_f=ab_f,
             ab_b=ab_b, k_k=k_k, k_a=k_a, r_k=r_k, g_up=g_up, lnx_w=lnx_w, lnx_b=lnx_b, q_norm_g=q_norm_g,
             k_norm_g=k_norm_g, w_br_rwkv=w_br_rwkv, w_br_attn=w_br_attn, w_out=w_out, w_router=w_router,
             b_router=b_router, w_gu=w_gu, b_gu=b_gu, w_down=w_down, b_down=b_down, normf_g=normf_g)
    f = _prepare_weights(w)
    return (_run(x_prompt, c_prompt, w, f), _run(x_sample, c_sample, w, f))
```

```python
import functools
import math

import jax
import jax.numpy as jnp
from jax import lax
from jax.experimental import pallas as pl
from jax.experimental.pallas import tpu as pltpu

F32 = jnp.float32
BF16 = jnp.bfloat16

D = 1024
GRID_W = 64
NH = 8
HD = 64
RW = NH * HD
RW_COLS = 1920
Q_OFF, K_OFF, V_OFF, G_OFF, IN_COLS = 1920, 2432, 2560, 2688, 4736
DECAY_SCALE = math.exp(-0.5)
GN_EPS = 64e-5
EPS = 1e-6
ROPE_THETA = 10000.0
N_EXPERTS = 32
TOP_K = 4
D_FF = 1024
SWIGLU_LIMIT = 7.0
SWIGLU_ALPHA = 1.702

TM = 256
CHUNK = 64
WKV_BLK = 256
HG = 256
ATT_TK = 512
MOE_BM = 512
LANES = 128
NEG_BIG = -1e30


def _cp(sem, vmem_mb=48):
    return pltpu.CompilerParams(dimension_semantics=sem, vmem_limit_bytes=vmem_mb << 20)


def _split3(x):
    h = x.astype(BF16)
    r1 = x - h.astype(F32)
    m = r1.astype(BF16)
    lo = (r1 - m.astype(F32)).astype(BF16)
    return h, m, lo


def _dot(a, b):
    return jnp.dot(a, b, preferred_element_type=F32)


def _dot_nt(a, b):
    return lax.dot_general(a, b, (((1,), (1,)), ((), ())), preferred_element_type=F32)


def _ada_kernel(c_ref, w_ref, b_ref, o_ref):
    c = c_ref[...]
    s = c * jax.nn.sigmoid(c)
    sh, sm, sl = _split3(s)
    wh, wm, wl = _split3(w_ref[...])
    acc = _dot(sh, wh) + (_dot(sh, wm) + _dot(sm, wh)) + (_dot(sm, wm) + _dot(sh, wl) + _dot(sl, wh))
    o_ref[...] = acc + b_ref[...]


def _ada(c8, w_ada, b_ada):
    n_mod = w_ada.shape[1] // D
    return pl.pallas_call(
        _ada_kernel,
        grid=(n_mod,),
        in_specs=[pl.BlockSpec((8, D), lambda j: (0, 0)),
                  pl.BlockSpec((D, D), lambda j: (0, j)),
                  pl.BlockSpec((1, D), lambda j: (0, j))],
        out_specs=pl.BlockSpec((8, D), lambda j: (0, j)),
        out_shape=jax.ShapeDtypeStruct((8, n_mod * D), F32),
        compiler_params=_cp(("arbitrary",)),
        name="ada",
    )(c8, w_ada, b_ada.reshape(1, -1))


def _rope(x, c, s, even):
    swap = jnp.where(even, pltpu.roll(x, LANES - 1, axis=1), pltpu.roll(x, 1, axis=1))
    return x * c + swap * s


def _inproj_kernel(x_ref, sc_ref, sh_ref, g_ref, w_ref, cos_ref, sin_ref, qg_ref, kg_ref, bd_ref,
                   prw_ref, q_ref, k_ref, v_ref, gate_ref):
    x = x_ref[...]
    ms = jnp.mean(x * x, axis=-1, keepdims=True)
    h = x * lax.rsqrt(ms + EPS) * g_ref[...]
    h = h * (1.0 + sc_ref[...]) + sh_ref[...]
    hb = h.astype(BF16)
    prw_ref[...] = _dot(hb, w_ref[:, 0:RW_COLS])
    gate_ref[...] = jax.nn.sigmoid(_dot(hb, w_ref[:, G_OFF:IN_COLS])).astype(BF16)

    cos = cos_ref[...]
    sin = sin_ref[...]
    bd = bd_ref[...]
    lane = lax.broadcasted_iota(jnp.int32, (TM, LANES), 1)
    even = (lane & 1) == 0
    low = lane < HD

    def norm_rope(slab, gain):
        ssq = _dot((slab * slab).astype(BF16), bd)
        return _rope(slab * lax.rsqrt(ssq * (1.0 / HD) + EPS) * gain, cos, sin, even)

    kr = norm_rope(_dot(hb, w_ref[:, K_OFF:V_OFF]), kg_ref[...])
    k_ref[0] = jnp.where(low, kr, 0.0).astype(BF16)
    k_ref[1] = jnp.where(low, pltpu.roll(kr, HD, axis=1), 0.0).astype(BF16)
    vf = _dot(hb, w_ref[:, V_OFF:G_OFF])
    one_col = jnp.where(lane == HD, 1.0, 0.0)
    v_ref[0] = jnp.where(low, vf, one_col).astype(BF16)
    v_ref[1] = jnp.where(low, pltpu.roll(vf, HD, axis=1), one_col).astype(BF16)

    qf = _dot(hb, w_ref[:, Q_OFF:K_OFF])
    scale = HD ** -0.5 * math.log2(math.e)
    for j in range(NH // 2):
        slab = norm_rope(qf[:, j * LANES:(j + 1) * LANES], qg_ref[...]) * scale
        g, ha = (2 * j) // 4, (2 * j) % 4
        q_ref[g, ha * TM:(ha + 1) * TM, :] = jnp.where(low, slab, 0.0).astype(BF16)
        q_ref[g, (ha + 1) * TM:(ha + 2) * TM, :] = jnp.where(low, pltpu.roll(slab, HD, axis=1), 0.0).astype(BF16)


def _inproj(x2, sc1, sh1, g1, w_in_b, cos_t, sin_t, qg, kg, bd128, n_tok):
    n = x2.shape[0]
    tpb = n_tok // TM
    row = lambda i: (i, 0)
    per_b = lambda i: (i // tpb, 0, 0)
    const2 = lambda i: (0, 0)
    tab = lambda i: (i % tpb, 0)
    return pl.pallas_call(
        _inproj_kernel,
        grid=(n // TM,),
        in_specs=[pl.BlockSpec((TM, D), row),
                  pl.BlockSpec((None, 1, D), per_b),
                  pl.BlockSpec((None, 1, D), per_b),
                  pl.BlockSpec((1, D), const2),
                  pl.BlockSpec((D, IN_COLS), const2),
                  pl.BlockSpec((TM, LANES), tab),
                  pl.BlockSpec((TM, LANES), tab),
                  pl.BlockSpec((1, LANES), const2),
                  pl.BlockSpec((1, LANES), const2),
                  pl.BlockSpec((LANES, LANES), const2)],
        out_specs=[pl.BlockSpec((TM, RW_COLS), row),
                   pl.BlockSpec((None, 2, 4 * TM, LANES), lambda i: (i, 0, 0, 0)),
                   pl.BlockSpec((2, TM, LANES), lambda i: (0, i, 0)),
                   pl.BlockSpec((2, TM, LANES), lambda i: (0, i, 0)),
                   pl.BlockSpec((TM, 2 * D), row)],
        out_shape=[jax.ShapeDtypeStruct((n, RW_COLS), F32),
                   jax.ShapeDtypeStruct((n // TM, 2, 4 * TM, LANES), BF16),
                   jax.ShapeDtypeStruct((2, n, LANES), BF16),
                   jax.ShapeDtypeStruct((2, n, LANES), BF16),
                   jax.ShapeDtypeStruct((n, 2 * D), BF16)],
        compiler_params=_cp(("arbitrary",), 56),
        name="inproj",
    )(x2, sc1, sh1, g1, w_in_b, cos_t, sin_t, qg, kg, bd128)


def _prep_kernel(p_ref, pprev_ref, pnext_ref, mup_ref, mun_ref, wdec_ref, wicl_ref, gup_ref,
                 w0_ref, a0_ref, kk_ref, ka_ref, rk_ref, bd_ref,
                 r_o, v_o, kk_o, lwf_o, lwb_o, kf_o, kb_o, bf_o, bb_o, go_o, bv_o, *, tpb):
    i = pl.program_id(0)
    ib = i % tpb
    p = p_ref[...]
    row = lax.broadcasted_iota(jnp.int32, (TM, 1), 0)
    hp = jnp.where(ib == 0, 0.0, pprev_ref[7:8, :])
    hn = jnp.where(ib == tpb - 1, 0.0, pnext_ref[0:1, :])
    prev = jnp.where(row == 0, hp, pltpu.roll(p, 1, axis=0))
    nxt = jnp.where(row == TM - 1, hn, pltpu.roll(p, TM - 1, axis=0))
    ps = p + mup_ref[...] * (prev - p) + mun_ref[...] * (nxt - p)

    r = ps[:, 0:RW]
    k = ps[:, RW:2 * RW]
    v = ps[:, 2 * RW:3 * RW]
    dlo = ps[:, 3 * RW:3 * RW + LANES]
    alo = ps[:, 3 * RW + LANES:3 * RW + 2 * LANES]
    glo = ps[:, 3 * RW + 2 * LANES:3 * RW + 3 * LANES]
    bd = bd_ref[...]

    dec = _dot(jnp.tanh(dlo).astype(BF16), wdec_ref[...]) + w0_ref[...]
    icl = _dot(alo.astype(BF16), wicl_ref[...]) + a0_ref[...]
    lw = -DECAY_SCALE * jax.nn.sigmoid(dec)
    a = jax.nn.sigmoid(icl)
    g_out = _dot(jax.nn.sigmoid(glo).astype(BF16), gup_ref[...])

    kk = k * kk_ref[...]
    ssq = _dot((kk * kk).astype(BF16), bd)
    kk = kk * lax.rsqrt(jnp.maximum(ssq, 1e-24))
    ka = ka_ref[...]
    a_f = a[:, 0:RW]
    a_b = a[:, RW:2 * RW]
    k_f = k * (1.0 + (a_f - 1.0) * ka)
    k_b = k * (1.0 + (a_b - 1.0) * ka)
    bonus = _dot((r * (k_f + k_b) * rk_ref[...]).astype(BF16), bd)

    r_o[...] = r
    v_o[...] = v
    kk_o[...] = kk
    lwf_o[...] = lw[:, 0:RW]
    lwb_o[...] = lw[:, RW:2 * RW]
    kf_o[...] = k_f
    kb_o[...] = k_b
    bf_o[...] = a_f * kk
    bb_o[...] = a_b * kk
    go_o[...] = g_out
    bv_o[...] = bonus * v


def _prep(p_rw, mup, mun, wdec, wicl, gup, w0, a0, k_k, k_a, r_k, bd512, n_tok):
    n = p_rw.shape[0]
    tpb = n_tok // TM
    nt8 = n // 8
    row = lambda i: (i, 0)
    c2 = lambda i: (0, 0)
    out = jax.ShapeDtypeStruct((n, RW), F32)
    return pl.pallas_call(
        functools.partial(_prep_kernel, tpb=tpb),
        grid=(n // TM,),
        in_specs=[pl.BlockSpec((TM, RW_COLS), row),
                  pl.BlockSpec((8, RW_COLS), lambda i: (jnp.maximum(i * (TM // 8) - 1, 0), 0)),
                  pl.BlockSpec((8, RW_COLS), lambda i: (jnp.minimum((i + 1) * (TM // 8), nt8 - 1), 0)),
                  pl.BlockSpec((1, RW_COLS), c2), pl.BlockSpec((1, RW_COLS), c2),
                  pl.BlockSpec((LANES, 2 * RW), c2), pl.BlockSpec((LANES, 2 * RW), c2),
                  pl.BlockSpec((LANES, RW), c2),
                  pl.BlockSpec((1, 2 * RW), c2), pl.BlockSpec((1, 2 * RW), c2),
                  pl.BlockSpec((1, RW), c2), pl.BlockSpec((1, RW), c2), pl.BlockSpec((1, RW), c2),
                  pl.BlockSpec((RW, RW), c2)],
        out_specs=[pl.BlockSpec((TM, RW), row)] * 11,
        out_shape=[out] * 11,
        compiler_params=_cp(("arbitrary",)),
        name="rwkv_prep",
    )(p_rw, p_rw, p_rw, mup, mun, wdec, wicl, gup, w0, a0, k_k, k_a, r_k, bd512)


def _bd(x, bdmask):
    return jnp.where(bdmask, jnp.concatenate([x, x, x, x], axis=0), 0.0)


def _wkv_pre(sl, r_ref, v_ref, kk_ref, lw_ref, k_ref, b_ref, tri_ref, rev, cst):
    strict, incl, eye_w, bdmask = cst
    r = r_ref[sl, :]
    v = v_ref[sl, :]
    kk = kk_ref[sl, :]
    lw = lw_ref[sl, :]
    k = k_ref[sl, :]
    b = b_ref[sl, :]
    tri = tri_ref[...]
    l3 = _split3(lw)
    cs = _dot(tri, l3[0]) + _dot(tri, l3[1]) + _dot(tri, l3[2])
    yield
    tot = cs[0:1, :] if rev else cs[CHUNK - 1:CHUNK, :]
    e_neg = jnp.exp(-cs)
    e_end = jnp.exp(tot - cs)
    rq = r * jnp.exp(cs)
    kq = kk * jnp.exp(cs - lw)

    lhs1 = jnp.concatenate([kq, rq], axis=0).astype(BF16)
    rhs1 = jnp.concatenate([_bd(b * e_neg, bdmask), _bd(k * e_neg, bdmask)], axis=0).astype(BF16)
    a_all = _dot_nt(lhs1, rhs1)
    yield
    n_ub = jnp.where(strict, a_all[0:CHUNK, 0:HG], 0.0)
    a_uk = jnp.where(strict, a_all[0:CHUNK, HG:2 * HG], 0.0)
    a_r = jnp.where(jnp.concatenate([incl, incl], axis=1), a_all[CHUNK:2 * CHUNK, :], 0.0).astype(BF16)

    t_inv = jnp.where(eye_w, 1.0, 0.0) - n_ub
    pw = n_ub
    for _ in range(int(math.log2(CHUNK)) - 1):
        pw = _dot(pw.astype(BF16), _bd(pw, bdmask).astype(BF16))
        yield
        t_inv = t_inv + _dot(t_inv.astype(BF16), _bd(pw, bdmask).astype(BF16))
        yield

    vb = _bd(v, bdmask).astype(BF16)
    x0 = _dot(a_uk.astype(BF16), vb)
    yield
    tb = t_inv.astype(BF16)
    u0 = -_dot(tb, _bd(x0, bdmask).astype(BF16))
    pm = _dot(tb, _bd(kq, bdmask).astype(BF16))
    yield
    lhs_t = jnp.concatenate([b * e_end, k * e_end], axis=0).T.astype(BF16)
    wcol = jnp.exp(jnp.broadcast_to(tot, (LANES, HG)).T)
    return dict(u0=u0, pmrq=jnp.concatenate([pm, rq], axis=0).astype(BF16), a_r=a_r, vb=vb, v=v.astype(BF16),
                lhs_t=lhs_t, wcol=jnp.concatenate([wcol, wcol], axis=1))


def _wkv_seq(sl, pre, st_ref, o_ref, bdmask):
    st = st_ref[...]
    g = _dot(pre["pmrq"], st.astype(BF16))
    yield
    u = pre["u0"] - g[0:CHUNK]
    upd = _dot(pre["lhs_t"], jnp.concatenate([u.astype(BF16), pre["v"]], axis=0))
    yield
    st_ref[...] = st * pre["wcol"] + jnp.where(bdmask, upd, 0.0)
    o_ref[sl, :] = g[CHUNK:2 * CHUNK] + _dot(
        pre["a_r"], jnp.concatenate([_bd(u, bdmask).astype(BF16), pre["vb"]], axis=0))


def _lockstep(gens):
    results = [None] * len(gens)
    live = list(range(len(gens)))
    while live:
        for i in list(live):
            try:
                next(gens[i])
            except StopIteration as stop:
                results[i] = stop.value
                live.remove(i)
    return results


def _wkv_kernel(rf, vf, kkf, lwf, kf, bf, rb, vb, kkb, lwb, kb, bb, trif, trib, of, ob, stf, stb):
    @pl.when(pl.program_id(2) == 0)
    def _():
        stf[...] = jnp.zeros_like(stf)
        stb[...] = jnp.zeros_like(stb)

    row = lax.broadcasted_iota(jnp.int32, (CHUNK, HG), 0)
    s_idx = lax.broadcasted_iota(jnp.int32, (CHUNK, HG), 1) & (HD - 1)
    r2 = lax.broadcasted_iota(jnp.int32, (HG, HG), 0) >> 6
    c2 = lax.broadcasted_iota(jnp.int32, (HG, HG), 1) >> 6
    bdmask = r2 == c2
    eye_w = s_idx == row
    cst_f = (s_idx < row, s_idx <= row, eye_w, bdmask)
    cst_b = (s_idx > row, s_idx >= row, eye_w, bdmask)
    nc = WKV_BLK // CHUNK
    sls = [pl.ds(c * CHUNK, CHUNK) for c in range(nc)]
    pre = _lockstep([_wkv_pre(sl, rf, vf, kkf, lwf, kf, bf, trif, False, cst_f) for sl in sls]
                    + [_wkv_pre(sl, rb, vb, kkb, lwb, kb, bb, trib, True, cst_b) for sl in sls])
    for c in range(nc):
        _lockstep([_wkv_seq(sls[c], pre[c], stf, of, bdmask),
                   _wkv_seq(sls[nc - 1 - c], pre[nc + nc - 1 - c], stb, ob, bdmask)])


def _wkv(r, v, kk, lwf, lwb, kf, kb, bf, bb, trif, trib, bsz, n_tok):
    nb = n_tok // WKV_BLK
    sh3 = lambda a: a.reshape(bsz, n_tok, RW)
    fwd = pl.BlockSpec((None, WKV_BLK, HG), lambda b, g, i: (b, i, g))
    bwd = pl.BlockSpec((None, WKV_BLK, HG), lambda b, g, i: (b, nb - 1 - i, g))
    tri = pl.BlockSpec((CHUNK, CHUNK), lambda b, g, i: (0, 0))
    out = jax.ShapeDtypeStruct((bsz, n_tok, RW), F32)
    of, ob = pl.pallas_call(
        _wkv_kernel,
        grid=(bsz, RW // HG, nb),
        in_specs=[fwd] * 6 + [bwd] * 6 + [tri, tri],
        out_specs=[fwd, bwd],
        out_shape=[out, out],
        scratch_shapes=[pltpu.VMEM((HG, HG), F32), pltpu.VMEM((HG, HG), F32)],
        compiler_params=_cp(("arbitrary", "arbitrary", "arbitrary")),
        name="wkv",
    )(sh3(r), sh3(v), sh3(kk), sh3(lwf), sh3(kf), sh3(bf),
      sh3(r), sh3(v), sh3(kk), sh3(lwb), sh3(kb), sh3(bb), trif, trib)
    return of.reshape(-1, RW), ob.reshape(-1, RW)


def _attn_kernel(q_ref, k_ref, v_ref, o_ref, m_scr, acc_scr, *, n_kb, tk):
    q = q_ref[...]
    m_scr[...] = jnp.full_like(m_scr, NEG_BIG)
    acc_scr[...] = jnp.zeros_like(acc_scr)

    def body(kb, carry):
        sl = pl.ds(pl.multiple_of(kb * tk, tk), tk)
        s = _dot_nt(q, k_ref[sl, :])
        m_prev = m_scr[...]
        m_new = jnp.maximum(m_prev, jnp.max(s, axis=1, keepdims=True))
        alpha = jnp.exp2(m_prev - m_new)
        p = jnp.exp2(s - jnp.tile(m_new, (1, tk // LANES)))
        acc_scr[...] = alpha * acc_scr[...] + _dot(p.astype(BF16), v_ref[sl, :])
        m_scr[...] = m_new
        return carry

    lax.fori_loop(0, n_kb, body, 0, unroll=2 if n_kb % 2 == 0 else 1)
    acc = acc_scr[...]
    o = acc / acc[:, HD:HD + 1]
    low = lax.broadcasted_iota(jnp.int32, (TM, LANES), 1) < HD
    for pr in range(2):
        oa = o[(2 * pr) * TM:(2 * pr + 1) * TM, :]
        ob = o[(2 * pr + 1) * TM:(2 * pr + 2) * TM, :]
        o_ref[:, pr * LANES:(pr + 1) * LANES] = jnp.where(low, oa, pltpu.roll(ob, HD, axis=1)).astype(BF16)


def _attention(q_st, k, v, bsz, n_tok):
    nq = n_tok // TM
    tk = min(ATT_TK, n_tok)
    q5 = q_st.reshape(bsz, nq, 2, 4 * TM, LANES)
    k4 = k.reshape(2, bsz, n_tok, LANES)
    v4 = v.reshape(2, bsz, n_tok, LANES)
    kv = pl.BlockSpec((None, None, n_tok, LANES), lambda b, g, i: (g, b, 0, 0))
    o = pl.pallas_call(
        functools.partial(_attn_kernel, n_kb=n_tok // tk, tk=tk),
        grid=(bsz, 2, nq),
        in_specs=[pl.BlockSpec((None, None, None, 4 * TM, LANES), lambda b, g, i: (b, i, g, 0, 0)), kv, kv],
        out_specs=pl.BlockSpec((None, TM, 2 * LANES), lambda b, g, i: (b, i, g)),
        out_shape=jax.ShapeDtypeStruct((bsz, n_tok, RW), BF16),
        scratch_shapes=[pltpu.VMEM((4 * TM, LANES), F32)] * 2,
        compiler_params=_cp(("arbitrary", "arbitrary", "arbitrary"), 56),
        name="attention",
    )(q5, k4, v4)
    return o.reshape(-1, RW)


def _merge_kernel(of_ref, ob_ref, bv_ref, go_ref, oat_ref, gate_ref, x_ref, gt1_ref, sc2_ref, sh2_ref,
                  lnw_ref, lnb_ref, g2_ref, wbr_ref, wba_ref, wout_ref, wrh_ref, wrl_ref, br_ref,
                  bd_ref, tri_ref, x1_ref, h2_ref, route_ref, cnt_ref, carry):
    @pl.when(pl.program_id(0) == 0)
    def _():
        carry[...] = jnp.zeros_like(carry)

    bd = bd_ref[...]
    o = of_ref[...] + ob_ref[...]
    oh, om, _ = _split3(o)
    mu = (_dot(oh, bd) + _dot(om, bd)) * (1.0 / HD)
    d = o - mu
    dh, dm, _ = _split3(d * d)
    var = (_dot(dh, bd) + _dot(dm, bd)) * (1.0 / HD)
    on = d * lax.rsqrt(var + GN_EPS) * lnw_ref[...] + lnb_ref[...]
    o_rw = (on + bv_ref[...]) * go_ref[...]
    br = _dot(o_rw.astype(BF16), wbr_ref[...])
    ba = _dot(oat_ref[...], wba_ref[...])
    gates = gate_ref[...]
    merged = gates[:, 0:D].astype(F32) * br + gates[:, D:2 * D].astype(F32) * ba
    x1 = x_ref[...] + gt1_ref[...] * _dot(merged.astype(BF16), wout_ref[...])
    x1_ref[...] = x1
    ms = jnp.mean(x1 * x1, axis=-1, keepdims=True)
    h2 = x1 * lax.rsqrt(ms + EPS) * g2_ref[...]
    h2 = h2 * (1.0 + sc2_ref[...]) + sh2_ref[...]
    h2_ref[...] = h2

    hh, hm, hl = _split3(h2)
    wh = wrh_ref[...]
    wl = wrl_ref[...]
    logits = _dot(hh, wh) + (_dot(hh, wl) + _dot(hm, wh)) + (_dot(hm, wl) + _dot(hl, wh)) + br_ref[...]

    lane = lax.broadcasted_iota(jnp.int32, (TM, LANES), 1).astype(F32)
    cur = logits
    vals, idxs = [], []
    for _ in range(TOP_K):
        m = jnp.max(cur, axis=1, keepdims=True)
        ix = jnp.min(jnp.where(cur == m, lane, float(LANES)), axis=1, keepdims=True)
        vals.append(m)
        idxs.append(ix)
        cur = jnp.where(lane == ix, -jnp.inf, cur)
    es = [jnp.exp(vv - vals[0]) for vv in vals]
    den = es[0] + es[1] + es[2] + es[3]
    onehot = jnp.zeros((TM, LANES), F32)
    for ix in idxs:
        onehot = onehot + jnp.where(lane == ix, 1.0, 0.0)
    cnt = _dot(tri_ref[...], onehot.astype(BF16)) + carry[...]
    route = jnp.zeros((TM, LANES), F32)
    for j in range(TOP_K):
        rank = jnp.sum(jnp.where(lane == idxs[j], cnt, 0.0), axis=1, keepdims=True)
        route = jnp.where(lane == float(j), idxs[j], route)
        route = jnp.where(lane == float(TOP_K + j), rank, route)
        route = jnp.where(lane == float(2 * TOP_K + j), es[j] / den, route)
    route_ref[...] = route
    carry[...] = carry[...] + jnp.sum(onehot, axis=0, keepdims=True)
    cnt_ref[...] = jnp.broadcast_to(carry[...], (8, LANES))


def _merge(of, ob, bv, go, oat, gates, x2, gt1, sc2, sh2, lnw, lnb, g2, wbr, wba, wout, wrh, wrl, brt,
           bd512, tri_tm, n_tok):
    n = x2.shape[0]
    tpb = n_tok // TM
    row = lambda i: (i, 0)
    per_b = lambda i: (i // tpb, 0, 0)
    c2 = lambda i: (0, 0)
    rw = pl.BlockSpec((TM, RW), row)
    full = lambda a: pl.BlockSpec(a.shape, c2)
    mod = pl.BlockSpec((None, 1, D), per_b)
    return pl.pallas_call(
        _merge_kernel,
        grid=(n // TM,),
        in_specs=[rw, rw, rw, rw, pl.BlockSpec((TM, RW), row), pl.BlockSpec((TM, 2 * D), row),
                  pl.BlockSpec((TM, D), row), mod, mod, mod,
                  full(lnw), full(lnb), full(g2), full(wbr), full(wba), full(wout), full(wrh), full(wrl),
                  full(brt), full(bd512), full(tri_tm)],
        out_specs=[pl.BlockSpec((TM, D), row), pl.BlockSpec((TM, D), row), pl.BlockSpec((TM, LANES), row),
                   pl.BlockSpec((8, LANES), c2)],
        out_shape=[jax.ShapeDtypeStruct((n, D), F32), jax.ShapeDtypeStruct((n, D), F32),
                   jax.ShapeDtypeStruct((n, LANES), F32), jax.ShapeDtypeStruct((8, LANES), F32)],
        scratch_shapes=[pltpu.VMEM((1, LANES), F32)],
        compiler_params=_cp(("arbitrary",)),
        name="merge",
    )(of, ob, bv, go, oat, gates, x2, gt1, sc2, sh2, lnw, lnb, g2, wbr, wba, wout, wrh, wrl, brt,
      bd512, tri_tm)


def _dispatch_kernel(idx_ref, rank_ref, pstart_ref, h2_ref, xs_in_hbm, xs_hbm, sem):
    del xs_in_hbm

    def body(t, carry):
        for j in range(TOP_K):
            dst = pstart_ref[idx_ref[t * TOP_K + j]] + rank_ref[t * TOP_K + j]
            pltpu.make_async_copy(h2_ref.at[pl.ds(t, 1)], xs_hbm.at[pl.ds(dst, 1)], sem).start()
        return carry

    lax.fori_loop(0, TM, body, 0)
    for _ in range(TOP_K):
        pltpu.make_async_copy(h2_ref, xs_hbm.at[pl.ds(0, TM)], sem).wait()


def _dispatch(idx_flat, rank_flat, pstart, h2, xs0):
    n = h2.shape[0]
    smem_blk = pl.BlockSpec((TM * TOP_K,), lambda i: (i,), memory_space=pltpu.SMEM)
    return pl.pallas_call(
        _dispatch_kernel,
        grid=(n // TM,),
        in_specs=[smem_blk, smem_blk,
                  pl.BlockSpec(memory_space=pltpu.SMEM),
                  pl.BlockSpec((TM, D), lambda i: (i, 0)),
                  pl.BlockSpec(memory_space=pl.ANY)],
        out_specs=pl.BlockSpec(memory_space=pl.ANY),
        out_shape=jax.ShapeDtypeStruct(xs0.shape, xs0.dtype),
        scratch_shapes=[pltpu.SemaphoreType.DMA(())],
        input_output_aliases={4: 0},
        compiler_params=_cp(("arbitrary",)),
        name="moe_dispatch",
    )(idx_flat, rank_flat, pstart, h2, xs0)


def _expert_kernel(bexp_ref, nused_ref, xs_ref, wgu_ref, bgu_ref, wd_ref, bdn_ref, ys_ref):
    i = pl.program_id(0)

    @pl.when(i < nused_ref[0])
    def _():
        gu = _dot(xs_ref[...].astype(BF16), wgu_ref[...]) + bgu_ref[...]
        g_lin = jnp.minimum(gu[:, 0:D_FF], SWIGLU_LIMIT)
        u_lin = jnp.clip(gu[:, D_FF:2 * D_FF], -SWIGLU_LIMIT, SWIGLU_LIMIT)
        act = (u_lin + 1.0) * (g_lin * jax.nn.sigmoid(SWIGLU_ALPHA * g_lin))
        ys_ref[...] = _dot(act.astype(BF16), wd_ref[...]) + bdn_ref[...]

    @pl.when(i >= nused_ref[0])
    def _():
        ys_ref[...] = jnp.zeros_like(ys_ref)


def _experts(blk_exp, nused, xs, wgu_b, bgu, wd_b, bdn):
    cap = xs.shape[0]
    grid_spec = pltpu.PrefetchScalarGridSpec(
        num_scalar_prefetch=2,
        grid=(cap // MOE_BM,),
        in_specs=[pl.BlockSpec((MOE_BM, D), lambda i, be, nu: (i, 0)),
                  pl.BlockSpec((None, D, 2 * D_FF), lambda i, be, nu: (be[i], 0, 0)),
                  pl.BlockSpec((None, 1, 2 * D_FF), lambda i, be, nu: (be[i], 0, 0)),
                  pl.BlockSpec((None, D_FF, D), lambda i, be, nu: (be[i], 0, 0)),
                  pl.BlockSpec((None, 1, D), lambda i, be, nu: (be[i], 0, 0))],
        out_specs=pl.BlockSpec((MOE_BM, D), lambda i, be, nu: (i, 0)),
    )
    return pl.pallas_call(
        _expert_kernel,
        grid_spec=grid_spec,
        out_shape=jax.ShapeDtypeStruct((cap, D), F32),
        compiler_params=_cp(("arbitrary",), 56),
        name="moe_experts",
    )(blk_exp, nused, xs, wgu_b, bgu, wd_b, bdn)


def _combine_kernel(idx_ref, rank_ref, pstart_ref, ys_hbm, route_ref, x1_ref, gt2_ref, gf_ref, o_ref,
                    buf, sem):
    def body(t, carry):
        for j in range(TOP_K):
            src = pstart_ref[idx_ref[t * TOP_K + j]] + rank_ref[t * TOP_K + j]
            pltpu.make_async_copy(ys_hbm.at[pl.ds(src, 1)], buf.at[j, pl.ds(t, 1)], sem).start()
        return carry

    lax.fori_loop(0, TM, body, 0)
    for j in range(TOP_K):
        pltpu.make_async_copy(ys_hbm.at[pl.ds(0, TM)], buf.at[j], sem).wait()
    route = route_ref[...]
    y = jnp.zeros((TM, D), F32)
    for j in range(TOP_K):
        y = y + route[:, 2 * TOP_K + j:2 * TOP_K + j + 1] * buf[j]
    x = x1_ref[...] + gt2_ref[...] * y
    ms = jnp.mean(x * x, axis=-1, keepdims=True)
    o_ref[...] = x * lax.rsqrt(ms + EPS) * gf_ref[...]


def _combine(idx_flat, rank_flat, pstart, ys, route, x1, gt2, gf, n_tok):
    n = x1.shape[0]
    tpb = n_tok // TM
    row = lambda i: (i, 0)
    smem_blk = pl.BlockSpec((TM * TOP_K,), lambda i: (i,), memory_space=pltpu.SMEM)
    return pl.pallas_call(
        _combine_kernel,
        grid=(n // TM,),
        in_specs=[smem_blk, smem_blk,
                  pl.BlockSpec(memory_space=pltpu.SMEM),
                  pl.BlockSpec(memory_space=pl.ANY),
                  pl.BlockSpec((TM, LANES), row),
                  pl.BlockSpec((TM, D), row),
                  pl.BlockSpec((None, 1, D), lambda i: (i // tpb, 0, 0)),
                  pl.BlockSpec((1, D), lambda i: (0, 0))],
        out_specs=pl.BlockSpec((TM, D), row),
        out_shape=jax.ShapeDtypeStruct((n, D), F32),
        scratch_shapes=[pltpu.VMEM((TOP_K, TM, D), F32), pltpu.SemaphoreType.DMA(())],
        compiler_params=_cp(("arbitrary",)),
        name="moe_combine",
    )(idx_flat, rank_flat, pstart, ys, route, x1, gt2, gf)


def _rope_tables(n_tok):
    t = jnp.arange(n_tok)
    row = (t // GRID_W).astype(F32)
    col = (t % GRID_W).astype(F32)
    nf = HD // 4
    freqs = ROPE_THETA ** (-jnp.arange(nf, dtype=F32) / nf)
    ang = jnp.concatenate([row[:, None] * freqs, col[:, None] * freqs], axis=-1)
    cos = jnp.repeat(jnp.cos(ang), 2, axis=-1)
    sin = jnp.repeat(jnp.sin(ang), 2, axis=-1)
    sign = jnp.tile(jnp.array([-1.0, 1.0], F32), HD // 2)
    return jnp.tile(cos, (1, 2)), jnp.tile(sin * sign, (1, 2))


def _block_diag2(a, b):
    z = jnp.zeros_like(a)
    return jnp.concatenate([jnp.concatenate([a, z], axis=1), jnp.concatenate([z, b], axis=1)], axis=0)


def _prepare_weights(w):
    f = {}
    f["w_in"] = w["w_in"][0].astype(BF16)
    f["g1"] = w["norm1_g"][0].reshape(1, D)
    f["g2"] = w["norm2_g"][0].reshape(1, D)
    f["qg"] = jnp.tile(w["q_norm_g"][0], 2).reshape(1, LANES)
    f["kg"] = jnp.tile(w["k_norm_g"][0], 2).reshape(1, LANES)
    ones = jnp.ones((HD, HD), F32)
    f["bd128"] = jnp.kron(jnp.eye(2, dtype=F32), ones).astype(BF16)
    f["bd512"] = jnp.kron(jnp.eye(NH, dtype=F32), ones).astype(BF16)
    f["mup"] = w["mu_prev"][0].reshape(1, RW_COLS)
    f["mun"] = w["mu_next"][0].reshape(1, RW_COLS)
    f["wdec"] = _block_diag2(w["wb_f"][0], w["wb_b"][0]).astype(BF16)
    f["wicl"] = _block_diag2(w["ab_f"][0], w["ab_b"][0]).astype(BF16)
    f["gup"] = w["g_up"][0].astype(BF16)
    f["w0"] = jnp.concatenate([w["w0_f"][0], w["w0_b"][0]]).reshape(1, 2 * RW)
    f["a0"] = jnp.concatenate([w["a0_f"][0], w["a0_b"][0]]).reshape(1, 2 * RW)
    f["k_k"] = w["k_k"][0].reshape(1, RW)
    f["k_a"] = w["k_a"][0].reshape(1, RW)
    f["r_k"] = w["r_k"][0].reshape(1, RW)
    ti = jnp.arange(CHUNK)
    f["trif"] = (ti[None, :] <= ti[:, None]).astype(BF16)
    f["trib"] = (ti[None, :] >= ti[:, None]).astype(BF16)
    tm = jnp.arange(TM)
    f["tri_tm"] = (tm[None, :] < tm[:, None]).astype(BF16)
    f["lnw"] = w["lnx_w"][0].reshape(1, RW)
    f["lnb"] = w["lnx_b"][0].reshape(1, RW)
    f["wbr"] = w["w_br_rwkv"][0].astype(BF16)
    f["wba"] = w["w_br_attn"][0].astype(BF16)
    f["wout"] = w["w_out"][0].astype(BF16)
    wr = jnp.pad(w["w_router"][0], ((0, 0), (0, LANES - N_EXPERTS)))
    wrh = wr.astype(BF16)
    f["wrh"] = wrh
    f["wrl"] = (wr - wrh.astype(F32)).astype(BF16)
    f["br"] = jnp.pad(w["b_router"][0], (0, LANES - N_EXPERTS), constant_values=NEG_BIG).reshape(1, LANES)
    f["wgu"] = w["w_gu"][0].astype(BF16)
    f["bgu"] = w["b_gu"][0].reshape(N_EXPERTS, 1, 2 * D_FF)
    f["wd"] = w["w_down"][0].astype(BF16)
    f["bdn"] = w["b_down"][0].reshape(N_EXPERTS, 1, D)
    f["gf"] = w["normf_g"].reshape(1, D)
    return f


def _run(x, c, w, f):
    bsz, n_tok, _ = x.shape
    n = bsz * n_tok
    x2 = x.reshape(n, D)
    c8 = jnp.pad(c, ((0, 8 - bsz), (0, 0)))
    mod = _ada(c8, w["w_ada"][0], w["b_ada"][0])[:bsz]
    sh1, sc1, gt1, sh2, sc2, gt2 = [m.reshape(bsz, 1, D) for m in jnp.split(mod, 6, axis=-1)]

    cos_t, sin_t = _rope_tables(n_tok)
    p_rw, q_st, k_att, v_att, gates = _inproj(x2, sc1, sh1, f["g1"], f["w_in"], cos_t, sin_t,
                                              f["qg"], f["kg"], f["bd128"], n_tok)
    r, v, kk, lwf, lwb, kf, kb, bf, bb, go, bv = _prep(
        p_rw, f["mup"], f["mun"], f["wdec"], f["wicl"], f["gup"], f["w0"], f["a0"],
        f["k_k"], f["k_a"], f["r_k"], f["bd512"], n_tok)
    of, ob = _wkv(r, v, kk, lwf, lwb, kf, kb, bf, bb, f["trif"], f["trib"], bsz, n_tok)
    o_at = _attention(q_st, k_att, v_att, bsz, n_tok)
    x1, h2, route, cnt = _merge(of, ob, bv, go, o_at, gates, x2, gt1, sc2, sh2, f["lnw"], f["lnb"], f["g2"],
                                f["wbr"], f["wba"], f["wout"], f["wrh"], f["wrl"], f["br"],
                                f["bd512"], f["tri_tm"], n_tok)

    counts = cnt[0, :N_EXPERTS].astype(jnp.int32)
    padded = (counts + MOE_BM - 1) // MOE_BM * MOE_BM
    pad_end = jnp.cumsum(padded)
    pstart = (pad_end - padded).astype(jnp.int32)
    nk = n * TOP_K
    nblk = -(-nk // MOE_BM) + N_EXPERTS
    blk_exp = jnp.minimum(jnp.searchsorted(pad_end, jnp.arange(nblk) * MOE_BM, side="right"),
                          N_EXPERTS - 1).astype(jnp.int32)
    nused = (pad_end[-1:] // MOE_BM).astype(jnp.int32)
    idx_flat = route[:, 0:TOP_K].astype(jnp.int32).reshape(nk)
    rank_flat = route[:, TOP_K:2 * TOP_K].astype(jnp.int32).reshape(nk)

    xs = _dispatch(idx_flat, rank_flat, pstart, h2, jnp.zeros((nblk * MOE_BM, D), F32))
    ys = _experts(blk_exp, nused, xs, f["wgu"], f["bgu"], f["wd"], f["bdn"])
    y = _combine(idx_flat, rank_flat, pstart, ys, route, x1, gt2, f["gf"], n_tok)
    return y.reshape(bsz, n_tok, D)


def kernel(x_prompt, x_sample, c_prompt, c_sample, norm1_g, norm2_g, w_ada, b_ada, w_in, mu_prev, mu_next, w0_f, w0_b, wb_f, wb_b, a0_f, a0_b, ab_f, ab_b, k_k, k_a, r_k, g_up, lnx_w, lnx_b, q_norm_g, k_norm_g, w_br_rwkv, w_br_attn, w_out, w_router, b_router, w_gu, b_gu, w_down, b_down, normf_g):
    w = dict(norm1_g=norm1_g, norm2_g=norm2_g, w_ada=w_ada, b_ada=b_ada, w_in=w_in, mu_prev=mu_prev,
             mu_next=mu_next, w0_f=w0_f, w0_b=w0_b, wb_f=wb_f, wb_b=wb_b, a0_f=a0_f, a0_b=a0_b, ab_f=ab_f,
             ab_b=ab_b, k_k=k_k, k_a=k_a, r_k=r_k, g_up=g_up, lnx_w=lnx_w, lnx_b=lnx_b, q_norm_g=q_norm_g,
             k_norm_g=k_norm_g, w_br_rwkv=w_br_rwkv, w_br_attn=w_br_attn, w_out=w_out, w_router=w_router,
             b_router=b_router, w_gu=w_gu, b_gu=b_gu, w_down=w_down, b_down=b_down, normf_g=normf_g)
    f = _prepare_weights(w)
    return (_run(x_prompt, c_prompt, w, f), _run(x_sample, c_sample, w, f))
```

```python
import functools
import math

import jax
import jax.numpy as jnp
from jax import lax
from jax.experimental import pallas as pl
from jax.experimental.pallas import tpu as pltpu

F32 = jnp.float32
BF16 = jnp.bfloat16

D = 1024
GRID_W = 64
NH = 8
HD = 64
RW = NH * HD
RW_COLS = 1920
Q_OFF, K_OFF, V_OFF, G_OFF, IN_COLS = 1920, 2432, 2560, 2688, 4736
DECAY_SCALE = math.exp(-0.5)
GN_EPS = 64e-5
EPS = 1e-6
ROPE_THETA = 10000.0
N_EXPERTS = 32
TOP_K = 4
D_FF = 1024
SWIGLU_LIMIT = 7.0
SWIGLU_ALPHA = 1.702

TM = 256
CHUNK = 64
WKV_BLK = 256
HG = 256
ATT_TK = 2048
MOE_BM = 512
DMA_UNROLL = 4
ROW_SUB = 8
LANES = 128
NEG_BIG = -1e30


def _cp(sem, vmem_mb=48):
    return pltpu.CompilerParams(dimension_semantics=sem, vmem_limit_bytes=vmem_mb << 20)


def _split3(x):
    h = x.astype(BF16)
    r1 = x - h.astype(F32)
    m = r1.astype(BF16)
    lo = (r1 - m.astype(F32)).astype(BF16)
    return h, m, lo


def _dot(a, b):
    return jnp.dot(a, b, preferred_element_type=F32)


def _dot_nt(a, b):
    return lax.dot_general(a, b, (((1,), (1,)), ((), ())), preferred_element_type=F32)


def _tile(r):
    return pl.ds(pl.multiple_of(r * ROW_SUB, ROW_SUB), ROW_SUB)


def _load_rows(ref, n_rows, lead=()):
    return jnp.concatenate([ref[lead + (pl.ds(sub, n_rows, stride=ROW_SUB), slice(None))]
                            for sub in range(ROW_SUB)], axis=1)


def _store_rows(ref, val, n_rows):
    for sub in range(ROW_SUB):
        ref[pl.ds(sub, n_rows, stride=ROW_SUB), :] = val[:, sub * LANES:(sub + 1) * LANES]


def _ada_kernel(c_ref, w_ref, b_ref, o_ref):
    c = c_ref[...]
    s = c * jax.nn.sigmoid(c)
    sh, sm, sl = _split3(s)
    wh, wm, wl = _split3(w_ref[...])
    acc = _dot(sh, wh) + (_dot(sh, wm) + _dot(sm, wh)) + (_dot(sm, wm) + _dot(sh, wl) + _dot(sl, wh))
    o_ref[...] = acc + b_ref[...]


def _ada(c8, w_ada, b_ada):
    n_mod = w_ada.shape[1] // D
    return pl.pallas_call(
        _ada_kernel,
        grid=(n_mod,),
        in_specs=[pl.BlockSpec((8, D), lambda j: (0, 0)),
                  pl.BlockSpec((D, D), lambda j: (0, j)),
                  pl.BlockSpec((1, D), lambda j: (0, j))],
        out_specs=pl.BlockSpec((8, D), lambda j: (0, j)),
        out_shape=jax.ShapeDtypeStruct((8, n_mod * D), F32),
        compiler_params=_cp(("arbitrary",)),
        name="ada",
    )(c8, w_ada, b_ada.reshape(1, -1))


def _rope(x, c, s, even):
    swap = jnp.where(even, pltpu.roll(x, LANES - 1, axis=1), pltpu.roll(x, 1, axis=1))
    return x * c + swap * s


def _inproj_kernel(x_ref, sc_ref, sh_ref, g_ref, w_ref, cos_ref, sin_ref, qg_ref, kg_ref, bd_ref,
                   prw_ref, q_ref, k_ref, v_ref, gate_ref):
    x = x_ref[...]
    ms = jnp.mean(x * x, axis=-1, keepdims=True)
    h = x * lax.rsqrt(ms + EPS) * g_ref[...]
    h = h * (1.0 + sc_ref[...]) + sh_ref[...]
    hb = h.astype(BF16)
    prw_ref[...] = _dot(hb, w_ref[:, 0:RW_COLS])
    gate_ref[...] = jax.nn.sigmoid(_dot(hb, w_ref[:, G_OFF:IN_COLS])).astype(BF16)

    cos = cos_ref[...]
    sin = sin_ref[...]
    bd = bd_ref[...]
    lane = lax.broadcasted_iota(jnp.int32, (TM, LANES), 1)
    even = (lane & 1) == 0
    low = lane < HD

    def norm_rope(slab, gain):
        ssq = _dot((slab * slab).astype(BF16), bd)
        return _rope(slab * lax.rsqrt(ssq * (1.0 / HD) + EPS) * gain, cos, sin, even)

    kr = norm_rope(_dot(hb, w_ref[:, K_OFF:V_OFF]), kg_ref[...])
    k_ref[0] = jnp.where(low, kr, 0.0).astype(BF16)
    k_ref[1] = jnp.where(low, pltpu.roll(kr, HD, axis=1), 0.0).astype(BF16)
    vf = _dot(hb, w_ref[:, V_OFF:G_OFF])
    one_col = jnp.where(lane == HD, 1.0, 0.0)
    v_ref[0] = jnp.where(low, vf, one_col).astype(BF16)
    v_ref[1] = jnp.where(low, pltpu.roll(vf, HD, axis=1), one_col).astype(BF16)

    qf = _dot(hb, w_ref[:, Q_OFF:K_OFF])
    scale = HD ** -0.5 * math.log2(math.e)
    for j in range(NH // 2):
        slab = norm_rope(qf[:, j * LANES:(j + 1) * LANES], qg_ref[...]) * scale
        g, ha = (2 * j) // 4, (2 * j) % 4
        q_ref[g, ha * TM:(ha + 1) * TM, :] = jnp.where(low, slab, 0.0).astype(BF16)
        q_ref[g, (ha + 1) * TM:(ha + 2) * TM, :] = jnp.where(low, pltpu.roll(slab, HD, axis=1), 0.0).astype(BF16)


def _inproj(x2, sc1, sh1, g1, w_in_b, cos_t, sin_t, qg, kg, bd128, n_tok):
    n = x2.shape[0]
    tpb = n_tok // TM
    row = lambda i: (i, 0)
    per_b = lambda i: (i // tpb, 0, 0)
    const2 = lambda i: (0, 0)
    tab = lambda i: (i % tpb, 0)
    return pl.pallas_call(
        _inproj_kernel,
        grid=(n // TM,),
        in_specs=[pl.BlockSpec((TM, D), row),
                  pl.BlockSpec((None, 1, D), per_b),
                  pl.BlockSpec((None, 1, D), per_b),
                  pl.BlockSpec((1, D), const2),
                  pl.BlockSpec((D, IN_COLS), const2),
                  pl.BlockSpec((TM, LANES), tab),
                  pl.BlockSpec((TM, LANES), tab),
                  pl.BlockSpec((1, LANES), const2),
                  pl.BlockSpec((1, LANES), const2),
                  pl.BlockSpec((LANES, LANES), const2)],
        out_specs=[pl.BlockSpec((TM, RW_COLS), row),
                   pl.BlockSpec((None, 2, 4 * TM, LANES), lambda i: (i, 0, 0, 0)),
                   pl.BlockSpec((2, TM, LANES), lambda i: (0, i, 0)),
                   pl.BlockSpec((2, TM, LANES), lambda i: (0, i, 0)),
                   pl.BlockSpec((TM, 2 * D), row)],
        out_shape=[jax.ShapeDtypeStruct((n, RW_COLS), F32),
                   jax.ShapeDtypeStruct((n // TM, 2, 4 * TM, LANES), BF16),
                   jax.ShapeDtypeStruct((2, n, LANES), BF16),
                   jax.ShapeDtypeStruct((2, n, LANES), BF16),
                   jax.ShapeDtypeStruct((n, 2 * D), BF16)],
        compiler_params=_cp(("arbitrary",), 56),
        name="inproj",
    )(x2, sc1, sh1, g1, w_in_b, cos_t, sin_t, qg, kg, bd128)


def _prep_kernel(p_ref, pprev_ref, pnext_ref, mup_ref, mun_ref, wdec_ref, wicl_ref, gup_ref,
                 w0_ref, a0_ref, kk_ref, ka_ref, rk_ref, bd_ref,
                 r_o, v_o, kk_o, lwf_o, lwb_o, kf_o, kb_o, bf_o, bb_o, go_o, bv_o, *, tpb):
    i = pl.program_id(0)
    ib = i % tpb
    p = p_ref[...]
    row = lax.broadcasted_iota(jnp.int32, (TM, 1), 0)
    hp = jnp.where(ib == 0, 0.0, pprev_ref[7:8, :])
    hn = jnp.where(ib == tpb - 1, 0.0, pnext_ref[0:1, :])
    prev = jnp.where(row == 0, hp, pltpu.roll(p, 1, axis=0))
    nxt = jnp.where(row == TM - 1, hn, pltpu.roll(p, TM - 1, axis=0))
    ps = p + mup_ref[...] * (prev - p) + mun_ref[...] * (nxt - p)

    r = ps[:, 0:RW]
    k = ps[:, RW:2 * RW]
    v = ps[:, 2 * RW:3 * RW]
    dlo = ps[:, 3 * RW:3 * RW + LANES]
    alo = ps[:, 3 * RW + LANES:3 * RW + 2 * LANES]
    glo = ps[:, 3 * RW + 2 * LANES:3 * RW + 3 * LANES]
    bd = bd_ref[...]

    dec = _dot(jnp.tanh(dlo).astype(BF16), wdec_ref[...]) + w0_ref[...]
    icl = _dot(alo.astype(BF16), wicl_ref[...]) + a0_ref[...]
    lw = -DECAY_SCALE * jax.nn.sigmoid(dec)
    a = jax.nn.sigmoid(icl)
    g_out = _dot(jax.nn.sigmoid(glo).astype(BF16), gup_ref[...])

    kk = k * kk_ref[...]
    ssq = _dot((kk * kk).astype(BF16), bd)
    kk = kk * lax.rsqrt(jnp.maximum(ssq, 1e-24))
    ka = ka_ref[...]
    a_f = a[:, 0:RW]
    a_b = a[:, RW:2 * RW]
    k_f = k * (1.0 + (a_f - 1.0) * ka)
    k_b = k * (1.0 + (a_b - 1.0) * ka)
    bonus = _dot((r * (k_f + k_b) * rk_ref[...]).astype(BF16), bd)

    r_o[...] = r
    v_o[...] = v
    kk_o[...] = kk
    lwf_o[...] = lw[:, 0:RW]
    lwb_o[...] = lw[:, RW:2 * RW]
    kf_o[...] = k_f
    kb_o[...] = k_b
    bf_o[...] = a_f * kk
    bb_o[...] = a_b * kk
    go_o[...] = g_out
    bv_o[...] = bonus * v


def _prep(p_rw, mup, mun, wdec, wicl, gup, w0, a0, k_k, k_a, r_k, bd512, n_tok):
    n = p_rw.shape[0]
    tpb = n_tok // TM
    nt8 = n // 8
    row = lambda i: (i, 0)
    c2 = lambda i: (0, 0)
    out = jax.ShapeDtypeStruct((n, RW), F32)
    return pl.pallas_call(
        functools.partial(_prep_kernel, tpb=tpb),
        grid=(n // TM,),
        in_specs=[pl.BlockSpec((TM, RW_COLS), row),
                  pl.BlockSpec((8, RW_COLS), lambda i: (jnp.maximum(i * (TM // 8) - 1, 0), 0)),
                  pl.BlockSpec((8, RW_COLS), lambda i: (jnp.minimum((i + 1) * (TM // 8), nt8 - 1), 0)),
                  pl.BlockSpec((1, RW_COLS), c2), pl.BlockSpec((1, RW_COLS), c2),
                  pl.BlockSpec((LANES, 2 * RW), c2), pl.BlockSpec((LANES, 2 * RW), c2),
                  pl.BlockSpec((LANES, RW), c2),
                  pl.BlockSpec((1, 2 * RW), c2), pl.BlockSpec((1, 2 * RW), c2),
                  pl.BlockSpec((1, RW), c2), pl.BlockSpec((1, RW), c2), pl.BlockSpec((1, RW), c2),
                  pl.BlockSpec((RW, RW), c2)],
        out_specs=[pl.BlockSpec((TM, RW), row)] * 11,
        out_shape=[out] * 11,
        compiler_params=_cp(("arbitrary",)),
        name="rwkv_prep",
    )(p_rw, p_rw, p_rw, mup, mun, wdec, wicl, gup, w0, a0, k_k, k_a, r_k, bd512)


def _bd(x, bdmask):
    return jnp.where(bdmask, jnp.concatenate([x, x, x, x], axis=0), 0.0)


def _wkv_pre(sl, r_ref, v_ref, kk_ref, lw_ref, k_ref, b_ref, tri_ref, rev, cst):
    strict, incl, eye_w, bdmask = cst
    r = r_ref[sl, :]
    v = v_ref[sl, :]
    kk = kk_ref[sl, :]
    lw = lw_ref[sl, :]
    k = k_ref[sl, :]
    b = b_ref[sl, :]
    tri = tri_ref[...]
    l3 = _split3(lw)
    cs = _dot(tri, l3[0]) + _dot(tri, l3[1]) + _dot(tri, l3[2])
    yield
    tot = cs[0:1, :] if rev else cs[CHUNK - 1:CHUNK, :]
    e_neg = jnp.exp(-cs)
    e_end = jnp.exp(tot - cs)
    rq = r * jnp.exp(cs)
    kq = kk * jnp.exp(cs - lw)

    lhs1 = jnp.concatenate([kq, rq], axis=0).astype(BF16)
    rhs1 = jnp.concatenate([_bd(b * e_neg, bdmask), _bd(k * e_neg, bdmask)], axis=0).astype(BF16)
    a_all = _dot_nt(lhs1, rhs1)
    yield
    n_ub = jnp.where(strict, a_all[0:CHUNK, 0:HG], 0.0)
    a_uk = jnp.where(strict, a_all[0:CHUNK, HG:2 * HG], 0.0)
    a_rb = jnp.where(incl, a_all[CHUNK:2 * CHUNK, 0:HG], 0.0).astype(BF16)
    a_rk = jnp.where(incl, a_all[CHUNK:2 * CHUNK, HG:2 * HG], 0.0)

    n_sq = int(math.log2(CHUNK))
    m_j = -n_ub
    t_inv = jnp.where(eye_w, 1.0, 0.0) + m_j
    m_j = _dot(m_j.astype(BF16), _bd(m_j, bdmask).astype(BF16))
    yield
    for j in range(1, n_sq):
        mb = _bd(m_j, bdmask).astype(BF16)
        if j < n_sq - 1:
            both = _dot(jnp.concatenate([t_inv, m_j], axis=0).astype(BF16), mb)
            t_inv = t_inv + both[0:CHUNK]
            m_j = both[CHUNK:2 * CHUNK]
        else:
            t_inv = t_inv + _dot(t_inv.astype(BF16), mb)
        yield

    av = _dot(jnp.concatenate([a_uk, a_rk], axis=0).astype(BF16), _bd(v, bdmask).astype(BF16))
    yield
    tb = t_inv.astype(BF16)
    u0 = -_dot(tb, _bd(av[0:CHUNK], bdmask).astype(BF16))
    pm = _dot(tb, _bd(kq, bdmask).astype(BF16))
    yield
    lhs_t = jnp.concatenate([b * e_end, k * e_end], axis=0).T.astype(BF16)
    wcol = jnp.exp(jnp.broadcast_to(tot, (LANES, HG)).T)
    return dict(u0=u0, pmrq=jnp.concatenate([pm, rq], axis=0).astype(BF16), a_rb=a_rb, yv=av[CHUNK:2 * CHUNK],
                v=v.astype(BF16), lhs_t=lhs_t, wcol=jnp.concatenate([wcol, wcol], axis=1))


def _wkv_seq(sl, pre, st_ref, o_ref, bdmask):
    st = st_ref[...]
    g = _dot(pre["pmrq"], st.astype(BF16))
    yield
    u = pre["u0"] - g[0:CHUNK]
    upd = _dot(pre["lhs_t"], jnp.concatenate([u.astype(BF16), pre["v"]], axis=0))
    yield
    st_ref[...] = st * pre["wcol"] + jnp.where(bdmask, upd, 0.0)
    o_ref[sl, :] = g[CHUNK:2 * CHUNK] + pre["yv"] + _dot(pre["a_rb"], _bd(u, bdmask).astype(BF16))


def _lockstep(gens):
    results = [None] * len(gens)
    live = list(range(len(gens)))
    while live:
        for i in list(live):
            try:
                next(gens[i])
            except StopIteration as stop:
                results[i] = stop.value
                live.remove(i)
    return results


def _wkv_kernel(rf, vf, kkf, lwf, kf, bf, rb, vb, kkb, lwb, kb, bb, trif, trib, of, ob, stf, stb):
    @pl.when(pl.program_id(2) == 0)
    def _():
        stf[...] = jnp.zeros_like(stf)
        stb[...] = jnp.zeros_like(stb)

    row = lax.broadcasted_iota(jnp.int32, (CHUNK, HG), 0)
    s_idx = lax.broadcasted_iota(jnp.int32, (CHUNK, HG), 1) & (HD - 1)
    r2 = lax.broadcasted_iota(jnp.int32, (HG, HG), 0) >> 6
    c2 = lax.broadcasted_iota(jnp.int32, (HG, HG), 1) >> 6
    bdmask = r2 == c2
    eye_w = s_idx == row
    cst_f = (s_idx < row, s_idx <= row, eye_w, bdmask)
    cst_b = (s_idx > row, s_idx >= row, eye_w, bdmask)
    nc = WKV_BLK // CHUNK
    sls = [pl.ds(c * CHUNK, CHUNK) for c in range(nc)]
    pre = _lockstep([_wkv_pre(sl, rf, vf, kkf, lwf, kf, bf, trif, False, cst_f) for sl in sls]
                    + [_wkv_pre(sl, rb, vb, kkb, lwb, kb, bb, trib, True, cst_b) for sl in sls])
    for c in range(nc):
        _lockstep([_wkv_seq(sls[c], pre[c], stf, of, bdmask),
                   _wkv_seq(sls[nc - 1 - c], pre[nc + nc - 1 - c], stb, ob, bdmask)])


def _wkv(r, v, kk, lwf, lwb, kf, kb, bf, bb, trif, trib, bsz, n_tok):
    nb = n_tok // WKV_BLK
    sh3 = lambda a: a.reshape(bsz, n_tok, RW)
    fwd = pl.BlockSpec((None, WKV_BLK, HG), lambda b, g, i: (b, i, g))
    bwd = pl.BlockSpec((None, WKV_BLK, HG), lambda b, g, i: (b, nb - 1 - i, g))
    tri = pl.BlockSpec((CHUNK, CHUNK), lambda b, g, i: (0, 0))
    out = jax.ShapeDtypeStruct((bsz, n_tok, RW), F32)
    of, ob = pl.pallas_call(
        _wkv_kernel,
        grid=(bsz, RW // HG, nb),
        in_specs=[fwd] * 6 + [bwd] * 6 + [tri, tri],
        out_specs=[fwd, bwd],
        out_shape=[out, out],
        scratch_shapes=[pltpu.VMEM((HG, HG), F32), pltpu.VMEM((HG, HG), F32)],
        compiler_params=_cp(("arbitrary", "arbitrary", "arbitrary")),
        name="wkv",
    )(sh3(r), sh3(v), sh3(kk), sh3(lwf), sh3(kf), sh3(bf),
      sh3(r), sh3(v), sh3(kk), sh3(lwb), sh3(kb), sh3(bb), trif, trib)
    return of.reshape(-1, RW), ob.reshape(-1, RW)


def _attn_kernel(q_ref, k_ref, v_ref, o_ref, m_scr, acc_scr, *, n_kb, tk):
    q = q_ref[...]
    m_scr[...] = jnp.full_like(m_scr, NEG_BIG)
    acc_scr[...] = jnp.zeros_like(acc_scr)

    def body(kb, carry):
        sl = pl.ds(pl.multiple_of(kb * tk, tk), tk)
        s = _dot_nt(q, k_ref[sl, :])
        m_prev = m_scr[...]
        m_new = jnp.maximum(m_prev, jnp.max(s, axis=1, keepdims=True))
        alpha = jnp.exp2(m_prev - m_new)
        p = jnp.exp2(s - jnp.tile(m_new, (1, tk // LANES)))
        acc_scr[...] = alpha * acc_scr[...] + _dot(p.astype(BF16), v_ref[sl, :])
        m_scr[...] = m_new
        return carry

    lax.fori_loop(0, n_kb, body, 0, unroll=2 if n_kb % 2 == 0 else 1)
    acc = acc_scr[...]
    o = acc / acc[:, HD:HD + 1]
    low = lax.broadcasted_iota(jnp.int32, (TM, LANES), 1) < HD
    for pr in range(2):
        oa = o[(2 * pr) * TM:(2 * pr + 1) * TM, :]
        ob = o[(2 * pr + 1) * TM:(2 * pr + 2) * TM, :]
        o_ref[:, pr * LANES:(pr + 1) * LANES] = jnp.where(low, oa, pltpu.roll(ob, HD, axis=1)).astype(BF16)


def _attention(q_st, k, v, bsz, n_tok):
    nq = n_tok // TM
    tk = min(ATT_TK, n_tok)
    q5 = q_st.reshape(bsz, nq, 2, 4 * TM, LANES)
    k4 = k.reshape(2, bsz, n_tok, LANES)
    v4 = v.reshape(2, bsz, n_tok, LANES)
    kv = pl.BlockSpec((None, None, n_tok, LANES), lambda b, g, i: (g, b, 0, 0))
    o = pl.pallas_call(
        functools.partial(_attn_kernel, n_kb=n_tok // tk, tk=tk),
        grid=(bsz, 2, nq),
        in_specs=[pl.BlockSpec((None, None, None, 4 * TM, LANES), lambda b, g, i: (b, i, g, 0, 0)), kv, kv],
        out_specs=pl.BlockSpec((None, TM, 2 * LANES), lambda b, g, i: (b, i, g)),
        out_shape=jax.ShapeDtypeStruct((bsz, n_tok, RW), BF16),
        scratch_shapes=[pltpu.VMEM((4 * TM, LANES), F32)] * 2,
        compiler_params=_cp(("arbitrary", "arbitrary", "arbitrary"), 56),
        name="attention",
    )(q5, k4, v4)
    return o.reshape(-1, RW)


def _merge_kernel(of_ref, ob_ref, bv_ref, go_ref, oat_ref, gate_ref, x_ref, gt1_ref, sc2_ref, sh2_ref,
                  lnw_ref, lnb_ref, g2_ref, wbr_ref, wba_ref, wout_ref, wrh_ref, wrl_ref, br_ref,
                  bd_ref, tri_ref, x1_ref, h2_ref, route_ref, cnt_ref, carry):
    @pl.when(pl.program_id(0) == 0)
    def _():
        carry[...] = jnp.zeros_like(carry)

    bd = bd_ref[...]
    o = of_ref[...] + ob_ref[...]
    oh, om, _ = _split3(o)
    mu = (_dot(oh, bd) + _dot(om, bd)) * (1.0 / HD)
    d = o - mu
    dh, dm, _ = _split3(d * d)
    var = (_dot(dh, bd) + _dot(dm, bd)) * (1.0 / HD)
    on = d * lax.rsqrt(var + GN_EPS) * lnw_ref[...] + lnb_ref[...]
    o_rw = (on + bv_ref[...]) * go_ref[...]
    br = _dot(o_rw.astype(BF16), wbr_ref[...])
    ba = _dot(oat_ref[...], wba_ref[...])
    gates = gate_ref[...]
    merged = gates[:, 0:D].astype(F32) * br + gates[:, D:2 * D].astype(F32) * ba
    x1 = x_ref[...] + gt1_ref[...] * _dot(merged.astype(BF16), wout_ref[...])
    x1_ref[...] = x1
    ms = jnp.mean(x1 * x1, axis=-1, keepdims=True)
    h2 = x1 * lax.rsqrt(ms + EPS) * g2_ref[...]
    h2 = h2 * (1.0 + sc2_ref[...]) + sh2_ref[...]
    _store_rows(h2_ref, h2, TM)

    hh, hm, hl = _split3(h2)
    wh = wrh_ref[...]
    wl = wrl_ref[...]
    logits = _dot(hh, wh) + (_dot(hh, wl) + _dot(hm, wh)) + (_dot(hm, wl) + _dot(hl, wh)) + br_ref[...]

    lane = lax.broadcasted_iota(jnp.int32, (TM, LANES), 1).astype(F32)
    cur = logits
    vals, idxs = [], []
    for _ in range(TOP_K):
        m = jnp.max(cur, axis=1, keepdims=True)
        ix = jnp.min(jnp.where(cur == m, lane, float(LANES)), axis=1, keepdims=True)
        vals.append(m)
        idxs.append(ix)
        cur = jnp.where(lane == ix, -jnp.inf, cur)
    es = [jnp.exp(vv - vals[0]) for vv in vals]
    den = es[0] + es[1] + es[2] + es[3]
    onehot = jnp.zeros((TM, LANES), F32)
    for ix in idxs:
        onehot = onehot + jnp.where(lane == ix, 1.0, 0.0)
    cnt = _dot(tri_ref[...], onehot.astype(BF16)) + carry[...]
    route = jnp.zeros((TM, LANES), F32)
    for j in range(TOP_K):
        rank = jnp.sum(jnp.where(lane == idxs[j], cnt, 0.0), axis=1, keepdims=True)
        route = jnp.where(lane == float(j), idxs[j], route)
        route = jnp.where(lane == float(TOP_K + j), rank, route)
        route = jnp.where(lane == float(2 * TOP_K + j), es[j] / den, route)
    route_ref[...] = route
    carry[...] = carry[...] + jnp.sum(onehot, axis=0, keepdims=True)
    cnt_ref[...] = jnp.broadcast_to(carry[...], (8, LANES))


def _merge(of, ob, bv, go, oat, gates, x2, gt1, sc2, sh2, lnw, lnb, g2, wbr, wba, wout, wrh, wrl, brt,
           bd512, tri_tm, n_tok):
    n = x2.shape[0]
    tpb = n_tok // TM
    row = lambda i: (i, 0)
    per_b = lambda i: (i // tpb, 0, 0)
    c2 = lambda i: (0, 0)
    rw = pl.BlockSpec((TM, RW), row)
    full = lambda a: pl.BlockSpec(a.shape, c2)
    mod = pl.BlockSpec((None, 1, D), per_b)
    return pl.pallas_call(
        _merge_kernel,
        grid=(n // TM,),
        in_specs=[rw, rw, rw, rw, pl.BlockSpec((TM, RW), row), pl.BlockSpec((TM, 2 * D), row),
                  pl.BlockSpec((TM, D), row), mod, mod, mod,
                  full(lnw), full(lnb), full(g2), full(wbr), full(wba), full(wout), full(wrh), full(wrl),
                  full(brt), full(bd512), full(tri_tm)],
        out_specs=[pl.BlockSpec((TM, D), row), pl.BlockSpec((TM * ROW_SUB, LANES), row),
                   pl.BlockSpec((TM, LANES), row),
                   pl.BlockSpec((8, LANES), c2)],
        out_shape=[jax.ShapeDtypeStruct((n, D), F32), jax.ShapeDtypeStruct((n * ROW_SUB, LANES), F32),
                   jax.ShapeDtypeStruct((n, LANES), F32), jax.ShapeDtypeStruct((8, LANES), F32)],
        scratch_shapes=[pltpu.VMEM((1, LANES), F32)],
        compiler_params=_cp(("arbitrary",)),
        name="merge",
    )(of, ob, bv, go, oat, gates, x2, gt1, sc2, sh2, lnw, lnb, g2, wbr, wba, wout, wrh, wrl, brt,
      bd512, tri_tm)


def _slots_kernel(route_ref, pstart_ref, o_ref):
    route = route_ref[...]
    ps = pstart_ref[...]
    lane = lax.broadcasted_iota(jnp.int32, (TM, LANES), 1).astype(F32)
    out = jnp.zeros((TM, LANES), F32)
    for j in range(TOP_K):
        off = jnp.sum(jnp.where(lane == route[:, j:j + 1], ps, 0.0), axis=1, keepdims=True)
        out = jnp.where(lane == float(j), off + route[:, TOP_K + j:TOP_K + j + 1], out)
    o_ref[...] = out.astype(jnp.int32)


def _slots(route, pstart_row):
    n = route.shape[0]
    return pl.pallas_call(
        _slots_kernel,
        grid=(n // TM,),
        in_specs=[pl.BlockSpec((TM, LANES), lambda i: (i, 0)), pl.BlockSpec((1, LANES), lambda i: (0, 0))],
        out_specs=pl.BlockSpec((TM, LANES), lambda i: (i, 0)),
        out_shape=jax.ShapeDtypeStruct((n, LANES), jnp.int32),
        compiler_params=_cp(("arbitrary",)),
        name="moe_slots",
    )(route, pstart_row)


def _dispatch_kernel(slot_ref, pend_ref, npad_ref, h2_ref, xs_hbm, zbuf, sem):
    @pl.when(pl.program_id(0) == 0)
    def _():
        zbuf[...] = jnp.zeros_like(zbuf)

        def fill_tail(blk, carry):
            first = pl.multiple_of(blk * (MOE_BM * ROW_SUB), MOE_BM * ROW_SUB)
            fill = pltpu.make_async_copy(zbuf, xs_hbm.at[pl.ds(first, MOE_BM * ROW_SUB)], sem)
            fill.start()
            fill.wait()
            return carry

        lax.fori_loop(pend_ref[N_EXPERTS - 1] // MOE_BM, xs_hbm.shape[0] // (MOE_BM * ROW_SUB), fill_tail, 0)
        for e in range(N_EXPERTS):
            @pl.when(npad_ref[e] > 0)
            def _():
                last = pl.multiple_of((pend_ref[e] - MOE_BM) * ROW_SUB, MOE_BM * ROW_SUB)
                fill = pltpu.make_async_copy(zbuf, xs_hbm.at[pl.ds(last, MOE_BM * ROW_SUB)], sem)
                fill.start()
                fill.wait()

    def body(t, carry):
        for j in range(TOP_K):
            pltpu.make_async_copy(h2_ref.at[_tile(t)], xs_hbm.at[_tile(slot_ref[t * TOP_K + j])], sem).start()
        return carry

    lax.fori_loop(0, TM, body, 0, unroll=DMA_UNROLL)
    for _ in range(TOP_K):
        pltpu.make_async_copy(h2_ref, xs_hbm.at[pl.ds(0, TM * ROW_SUB)], sem).wait()


def _dispatch(slot_flat, pad_end, padded, h2, cap):
    n = h2.shape[0] // ROW_SUB
    return pl.pallas_call(
        _dispatch_kernel,
        grid=(n // TM,),
        in_specs=[pl.BlockSpec((TM * TOP_K,), lambda i: (i,), memory_space=pltpu.SMEM),
                  pl.BlockSpec(memory_space=pltpu.SMEM),
                  pl.BlockSpec(memory_space=pltpu.SMEM),
                  pl.BlockSpec((TM * ROW_SUB, LANES), lambda i: (i, 0))],
        out_specs=pl.BlockSpec(memory_space=pl.ANY),
        out_shape=jax.ShapeDtypeStruct((cap * ROW_SUB, LANES), F32),
        scratch_shapes=[pltpu.VMEM((MOE_BM * ROW_SUB, LANES), F32), pltpu.SemaphoreType.DMA(())],
        compiler_params=_cp(("arbitrary",)),
        name="moe_dispatch",
    )(slot_flat, pad_end, padded, h2)


def _expert_kernel(bexp_ref, nused_ref, xs_ref, wgu_ref, bgu_ref, wd_ref, bdn_ref, ys_ref):
    i = pl.program_id(0)

    @pl.when(i < nused_ref[0])
    def _():
        gu = _dot(_load_rows(xs_ref, MOE_BM).astype(BF16), wgu_ref[...]) + bgu_ref[...]
        g_lin = jnp.minimum(gu[:, 0:D_FF], SWIGLU_LIMIT)
        u_lin = jnp.clip(gu[:, D_FF:2 * D_FF], -SWIGLU_LIMIT, SWIGLU_LIMIT)
        act = (u_lin + 1.0) * (g_lin * jax.nn.sigmoid(SWIGLU_ALPHA * g_lin))
        _store_rows(ys_ref, _dot(act.astype(BF16), wd_ref[...]) + bdn_ref[...], MOE_BM)

    @pl.when(i >= nused_ref[0])
    def _():
        ys_ref[...] = jnp.zeros_like(ys_ref)


def _experts(blk_exp, nused, xs, wgu_b, bgu, wd_b, bdn):
    cap = xs.shape[0] // ROW_SUB
    grid_spec = pltpu.PrefetchScalarGridSpec(
        num_scalar_prefetch=2,
        grid=(cap // MOE_BM,),
        in_specs=[pl.BlockSpec((MOE_BM * ROW_SUB, LANES), lambda i, be, nu: (jnp.minimum(i, nu[0] - 1), 0)),
                  pl.BlockSpec((None, D, 2 * D_FF), lambda i, be, nu: (be[i], 0, 0)),
                  pl.BlockSpec((None, 1, 2 * D_FF), lambda i, be, nu: (be[i], 0, 0)),
                  pl.BlockSpec((None, D_FF, D), lambda i, be, nu: (be[i], 0, 0)),
                  pl.BlockSpec((None, 1, D), lambda i, be, nu: (be[i], 0, 0))],
        out_specs=pl.BlockSpec((MOE_BM * ROW_SUB, LANES), lambda i, be, nu: (i, 0)),
    )
    return pl.pallas_call(
        _expert_kernel,
        grid_spec=grid_spec,
        out_shape=jax.ShapeDtypeStruct(xs.shape, F32),
        compiler_params=_cp(("arbitrary",), 56),
        name="moe_experts",
    )(blk_exp, nused, xs, wgu_b, bgu, wd_b, bdn)


def _combine_kernel(slot_ref, ys_hbm, route_ref, x1_ref, gt2_ref, gf_ref, o_ref, buf, sem):
    def body(t, carry):
        for j in range(TOP_K):
            pltpu.make_async_copy(ys_hbm.at[_tile(slot_ref[t * TOP_K + j])], buf.at[j, _tile(t)], sem).start()
        return carry

    lax.fori_loop(0, TM, body, 0, unroll=DMA_UNROLL)
    for j in range(TOP_K):
        pltpu.make_async_copy(ys_hbm.at[pl.ds(0, TM * ROW_SUB)], buf.at[j], sem).wait()
    route = route_ref[...]
    y = jnp.zeros((TM, D), F32)
    for j in range(TOP_K):
        y = y + route[:, 2 * TOP_K + j:2 * TOP_K + j + 1] * _load_rows(buf, TM, lead=(j,))
    x = x1_ref[...] + gt2_ref[...] * y
    ms = jnp.mean(x * x, axis=-1, keepdims=True)
    o_ref[...] = x * lax.rsqrt(ms + EPS) * gf_ref[...]


def _combine(slot_flat, ys, route, x1, gt2, gf, n_tok):
    n = x1.shape[0]
    tpb = n_tok // TM
    row = lambda i: (i, 0)
    return pl.pallas_call(
        _combine_kernel,
        grid=(n // TM,),
        in_specs=[pl.BlockSpec((TM * TOP_K,), lambda i: (i,), memory_space=pltpu.SMEM),
                  pl.BlockSpec(memory_space=pl.ANY),
                  pl.BlockSpec((TM, LANES), row),
                  pl.BlockSpec((TM, D), row),
                  pl.BlockSpec((None, 1, D), lambda i: (i // tpb, 0, 0)),
                  pl.BlockSpec((1, D), lambda i: (0, 0))],
        out_specs=pl.BlockSpec((TM, D), row),
        out_shape=jax.ShapeDtypeStruct((n, D), F32),
        scratch_shapes=[pltpu.VMEM((TOP_K, TM * ROW_SUB, LANES), F32), pltpu.SemaphoreType.DMA(())],
        compiler_params=_cp(("arbitrary",)),
        name="moe_combine",
    )(slot_flat, ys, route, x1, gt2, gf)


def _rope_tables(n_tok):
    t = jnp.arange(n_tok)
    row = (t // GRID_W).astype(F32)
    col = (t % GRID_W).astype(F32)
    nf = HD // 4
    freqs = ROPE_THETA ** (-jnp.arange(nf, dtype=F32) / nf)
    ang = jnp.concatenate([row[:, None] * freqs, col[:, None] * freqs], axis=-1)
    cos = jnp.repeat(jnp.cos(ang), 2, axis=-1)
    sin = jnp.repeat(jnp.sin(ang), 2, axis=-1)
    sign = jnp.tile(jnp.array([-1.0, 1.0], F32), HD // 2)
    return jnp.tile(cos, (1, 2)), jnp.tile(sin * sign, (1, 2))


def _block_diag2(a, b):
    z = jnp.zeros_like(a)
    return jnp.concatenate([jnp.concatenate([a, z], axis=1), jnp.concatenate([z, b], axis=1)], axis=0)


def _prepare_weights(w):
    f = {}
    f["w_in"] = w["w_in"][0].astype(BF16)
    f["g1"] = w["norm1_g"][0].reshape(1, D)
    f["g2"] = w["norm2_g"][0].reshape(1, D)
    f["qg"] = jnp.tile(w["q_norm_g"][0], 2).reshape(1, LANES)
    f["kg"] = jnp.tile(w["k_norm_g"][0], 2).reshape(1, LANES)
    ones = jnp.ones((HD, HD), F32)
    f["bd128"] = jnp.kron(jnp.eye(2, dtype=F32), ones).astype(BF16)
    f["bd512"] = jnp.kron(jnp.eye(NH, dtype=F32), ones).astype(BF16)
    f["mup"] = w["mu_prev"][0].reshape(1, RW_COLS)
    f["mun"] = w["mu_next"][0].reshape(1, RW_COLS)
    f["wdec"] = _block_diag2(w["wb_f"][0], w["wb_b"][0]).astype(BF16)
    f["wicl"] = _block_diag2(w["ab_f"][0], w["ab_b"][0]).astype(BF16)
    f["gup"] = w["g_up"][0].astype(BF16)
    f["w0"] = jnp.concatenate([w["w0_f"][0], w["w0_b"][0]]).reshape(1, 2 * RW)
    f["a0"] = jnp.concatenate([w["a0_f"][0], w["a0_b"][0]]).reshape(1, 2 * RW)
    f["k_k"] = w["k_k"][0].reshape(1, RW)
    f["k_a"] = w["k_a"][0].reshape(1, RW)
    f["r_k"] = w["r_k"][0].reshape(1, RW)
    ti = jnp.arange(CHUNK)
    f["trif"] = (ti[None, :] <= ti[:, None]).astype(BF16)
    f["trib"] = (ti[None, :] >= ti[:, None]).astype(BF16)
    tm = jnp.arange(TM)
    f["tri_tm"] = (tm[None, :] < tm[:, None]).astype(BF16)
    f["lnw"] = w["lnx_w"][0].reshape(1, RW)
    f["lnb"] = w["lnx_b"][0].reshape(1, RW)
    f["wbr"] = w["w_br_rwkv"][0].astype(BF16)
    f["wba"] = w["w_br_attn"][0].astype(BF16)
    f["wout"] = w["w_out"][0].astype(BF16)
    wr = jnp.pad(w["w_router"][0], ((0, 0), (0, LANES - N_EXPERTS)))
    wrh = wr.astype(BF16)
    f["wrh"] = wrh
    f["wrl"] = (wr - wrh.astype(F32)).astype(BF16)
    f["br"] = jnp.pad(w["b_router"][0], (0, LANES - N_EXPERTS), constant_values=NEG_BIG).reshape(1, LANES)
    f["wgu"] = w["w_gu"][0].astype(BF16)
    f["bgu"] = w["b_gu"][0].reshape(N_EXPERTS, 1, 2 * D_FF)
    f["wd"] = w["w_down"][0].astype(BF16)
    f["bdn"] = w["b_down"][0].reshape(N_EXPERTS, 1, D)
    f["gf"] = w["normf_g"].reshape(1, D)
    return f


def _run(x, c, w, f):
    bsz, n_tok, _ = x.shape
    n = bsz * n_tok
    x2 = x.reshape(n, D)
    c8 = jnp.pad(c, ((0, 8 - bsz), (0, 0)))
    mod = _ada(c8, w["w_ada"][0], w["b_ada"][0])[:bsz]
    sh1, sc1, gt1, sh2, sc2, gt2 = [m.reshape(bsz, 1, D) for m in jnp.split(mod, 6, axis=-1)]

    cos_t, sin_t = _rope_tables(n_tok)
    p_rw, q_st, k_att, v_att, gates = _inproj(x2, sc1, sh1, f["g1"], f["w_in"], cos_t, sin_t,
                                              f["qg"], f["kg"], f["bd128"], n_tok)
    r, v, kk, lwf, lwb, kf, kb, bf, bb, go, bv = _prep(
        p_rw, f["mup"], f["mun"], f["wdec"], f["wicl"], f["gup"], f["w0"], f["a0"],
        f["k_k"], f["k_a"], f["r_k"], f["bd512"], n_tok)
    of, ob = _wkv(r, v, kk, lwf, lwb, kf, kb, bf, bb, f["trif"], f["trib"], bsz, n_tok)
    o_at = _attention(q_st, k_att, v_att, bsz, n_tok)
    x1, h2, route, cnt = _merge(of, ob, bv, go, o_at, gates, x2, gt1, sc2, sh2, f["lnw"], f["lnb"], f["g2"],
                                f["wbr"], f["wba"], f["wout"], f["wrh"], f["wrl"], f["br"],
                                f["bd512"], f["tri_tm"], n_tok)

    counts = cnt[0, :N_EXPERTS].astype(jnp.int32)
    padded = ((counts + MOE_BM - 1) // MOE_BM * MOE_BM).astype(jnp.int32)
    pad_end = jnp.cumsum(padded).astype(jnp.int32)
    pstart_row = jnp.pad((pad_end - padded).astype(F32), (0, LANES - N_EXPERTS)).reshape(1, LANES)
    nk = n * TOP_K
    nblk = -(-nk // MOE_BM) + N_EXPERTS
    blk_first = (jnp.arange(nblk) * MOE_BM)[:, None]
    blk_exp = jnp.minimum(jnp.sum(pad_end[None, :] <= blk_first, axis=1), N_EXPERTS - 1).astype(jnp.int32)
    nused = (pad_end[-1:] // MOE_BM).astype(jnp.int32)

    slot_flat = _slots(route, pstart_row)[:, 0:TOP_K].reshape(nk)
    xs = _dispatch(slot_flat, pad_end, padded, h2, nblk * MOE_BM)
    ys = _experts(blk_exp, nused, xs, f["wgu"], f["bgu"], f["wd"], f["bdn"])
    y = _combine(slot_flat, ys, route, x1, gt2, f["gf"], n_tok)
    return y.reshape(bsz, n_tok, D)


def kernel(x_prompt, x_sample, c_prompt, c_sample, norm1_g, norm2_g, w_ada, b_ada, w_in, mu_prev, mu_next, w0_f, w0_b, wb_f, wb_b, a0_f, a0_b, ab_f, ab_b, k_k, k_a, r_k, g_up, lnx_w, lnx_b, q_norm_g, k_norm_g, w_br_rwkv, w_br_attn, w_out, w_router, b_router, w_gu, b_gu, w_down, b_down, normf_g):
    w = dict(norm1_g=norm1_g, norm2_g=norm2_g, w_ada=w_ada, b_ada=b_ada, w_in=w_in, mu_prev=mu_prev,
             mu_next=mu_next, w0_f=w0_f, w0_b=w0_b, wb_f=wb_f, wb_b=wb_b, a0_f=a0_f, a0_b=a0_b, ab_f=ab_f,
             ab_b=ab_b, k_k=k_k, k_a=k_a, r_k=r_k, g_up=g_up, lnx_w=lnx_w, lnx_b=lnx_b, q_norm_g=q_norm_g,
             k_norm_g=k_norm_g, w_br_rwkv=w_br_rwkv, w_br_attn=w_br_attn, w_out=w_out, w_router=w_router,
             b_router=b_router, w_gu=w_gu, b_gu=b_gu, w_down=w_down, b_down=b_down, normf_g=normf_g)
    f = _prepare_weights(w)
    return (_run(x_prompt, c_prompt, w, f), _run(x_sample, c_sample, w, f))
```

```python
import functools
import math

import jax
import jax.numpy as jnp
from jax import lax
from jax.experimental import pallas as pl
from jax.experimental.pallas import tpu as pltpu

F32 = jnp.float32
BF16 = jnp.bfloat16

D = 1024
GRID_W = 64
NH = 8
HD = 64
RW = NH * HD
RW_COLS = 1920
Q_OFF, K_OFF, V_OFF, G_OFF, IN_COLS = 1920, 2432, 2560, 2688, 4736
DECAY_SCALE = math.exp(-0.5)
GN_EPS = 64e-5
EPS = 1e-6
ROPE_THETA = 10000.0
N_EXPERTS = 32
TOP_K = 4
D_FF = 1024
SWIGLU_LIMIT = 7.0
SWIGLU_ALPHA = 1.702

TM = 256
CHUNK = 64
WKV_BLK = 256
HG = 256
ATT_TK = 2048
MOE_BM = 512
SLOT_TM = 1024
DMA_UNROLL = 4
ROW_SUB = 8
LANES = 128
NEG_BIG = -1e30


def _cp(sem, vmem_mb=48):
    return pltpu.CompilerParams(dimension_semantics=sem, vmem_limit_bytes=vmem_mb << 20)


def _split3(x):
    h = x.astype(BF16)
    r1 = x - h.astype(F32)
    m = r1.astype(BF16)
    lo = (r1 - m.astype(F32)).astype(BF16)
    return h, m, lo


def _dot(a, b):
    return jnp.dot(a, b, preferred_element_type=F32)


def _dot_nt(a, b):
    return lax.dot_general(a, b, (((1,), (1,)), ((), ())), preferred_element_type=F32)


def _tile(r):
    return pl.ds(pl.multiple_of(r * ROW_SUB, ROW_SUB), ROW_SUB)


def _load_rows(ref, n_rows, lead=()):
    return jnp.concatenate([ref[lead + (pl.ds(sub, n_rows, stride=ROW_SUB), slice(None))]
                            for sub in range(ROW_SUB)], axis=1)


def _store_rows(ref, val, n_rows):
    for sub in range(ROW_SUB):
        ref[pl.ds(sub, n_rows, stride=ROW_SUB), :] = val[:, sub * LANES:(sub + 1) * LANES]


def _ada_kernel(c_ref, w_ref, b_ref, o_ref):
    c = c_ref[...]
    s = c * jax.nn.sigmoid(c)
    sh, sm, sl = _split3(s)
    wh, wm, wl = _split3(w_ref[...])
    acc = _dot(sh, wh) + (_dot(sh, wm) + _dot(sm, wh)) + (_dot(sm, wm) + _dot(sh, wl) + _dot(sl, wh))
    o_ref[...] = acc + b_ref[...]


def _ada(c8, w_ada, b_ada):
    n_mod = w_ada.shape[1] // D
    return pl.pallas_call(
        _ada_kernel,
        grid=(n_mod,),
        in_specs=[pl.BlockSpec((8, D), lambda j: (0, 0)),
                  pl.BlockSpec((D, D), lambda j: (0, j)),
                  pl.BlockSpec((1, D), lambda j: (0, j))],
        out_specs=pl.BlockSpec((8, D), lambda j: (0, j)),
        out_shape=jax.ShapeDtypeStruct((8, n_mod * D), F32),
        compiler_params=_cp(("arbitrary",)),
        name="ada",
    )(c8, w_ada, b_ada.reshape(1, -1))


def _rope(x, c, s, even):
    swap = jnp.where(even, pltpu.roll(x, LANES - 1, axis=1), pltpu.roll(x, 1, axis=1))
    return x * c + swap * s


def _inproj_kernel(x_ref, sc_ref, sh_ref, g_ref, w_ref, cos_ref, sin_ref, qg_ref, kg_ref, bd_ref,
                   prw_ref, q_ref, k_ref, v_ref, gate_ref):
    x = x_ref[...]
    ms = jnp.mean(x * x, axis=-1, keepdims=True)
    h = x * lax.rsqrt(ms + EPS) * g_ref[...]
    h = h * (1.0 + sc_ref[...]) + sh_ref[...]
    hb = h.astype(BF16)
    prw_ref[...] = _dot(hb, w_ref[:, 0:RW_COLS])
    gate_ref[...] = jax.nn.sigmoid(_dot(hb, w_ref[:, G_OFF:IN_COLS])).astype(BF16)

    cos = cos_ref[...]
    sin = sin_ref[...]
    bd = bd_ref[...]
    lane = lax.broadcasted_iota(jnp.int32, (TM, LANES), 1)
    even = (lane & 1) == 0
    low = lane < HD

    def norm_rope(slab, gain):
        ssq = _dot((slab * slab).astype(BF16), bd)
        return _rope(slab * lax.rsqrt(ssq * (1.0 / HD) + EPS) * gain, cos, sin, even)

    kr = norm_rope(_dot(hb, w_ref[:, K_OFF:V_OFF]), kg_ref[...])
    k_ref[0] = jnp.where(low, kr, 0.0).astype(BF16)
    k_ref[1] = jnp.where(low, pltpu.roll(kr, HD, axis=1), 0.0).astype(BF16)
    vf = _dot(hb, w_ref[:, V_OFF:G_OFF])
    one_col = jnp.where(lane == HD, 1.0, 0.0)
    v_ref[0] = jnp.where(low, vf, one_col).astype(BF16)
    v_ref[1] = jnp.where(low, pltpu.roll(vf, HD, axis=1), one_col).astype(BF16)

    qf = _dot(hb, w_ref[:, Q_OFF:K_OFF])
    scale = HD ** -0.5 * math.log2(math.e)
    for j in range(NH // 2):
        slab = norm_rope(qf[:, j * LANES:(j + 1) * LANES], qg_ref[...]) * scale
        g, ha = (2 * j) // 4, (2 * j) % 4
        q_ref[g, ha * TM:(ha + 1) * TM, :] = jnp.where(low, slab, 0.0).astype(BF16)
        q_ref[g, (ha + 1) * TM:(ha + 2) * TM, :] = jnp.where(low, pltpu.roll(slab, HD, axis=1), 0.0).astype(BF16)


def _inproj(x2, sc1, sh1, g1, w_in_b, cos_t, sin_t, qg, kg, bd128, n_tok):
    n = x2.shape[0]
    tpb = n_tok // TM
    row = lambda i: (i, 0)
    per_b = lambda i: (i // tpb, 0, 0)
    const2 = lambda i: (0, 0)
    tab = lambda i: (i % tpb, 0)
    return pl.pallas_call(
        _inproj_kernel,
        grid=(n // TM,),
        in_specs=[pl.BlockSpec((TM, D), row),
                  pl.BlockSpec((None, 1, D), per_b),
                  pl.BlockSpec((None, 1, D), per_b),
                  pl.BlockSpec((1, D), const2),
                  pl.BlockSpec((D, IN_COLS), const2),
                  pl.BlockSpec((TM, LANES), tab),
                  pl.BlockSpec((TM, LANES), tab),
                  pl.BlockSpec((1, LANES), const2),
                  pl.BlockSpec((1, LANES), const2),
                  pl.BlockSpec((LANES, LANES), const2)],
        out_specs=[pl.BlockSpec((TM, RW_COLS), row),
                   pl.BlockSpec((None, 2, 4 * TM, LANES), lambda i: (i, 0, 0, 0)),
                   pl.BlockSpec((2, TM, LANES), lambda i: (0, i, 0)),
                   pl.BlockSpec((2, TM, LANES), lambda i: (0, i, 0)),
                   pl.BlockSpec((TM, 2 * D), row)],
        out_shape=[jax.ShapeDtypeStruct((n, RW_COLS), F32),
                   jax.ShapeDtypeStruct((n // TM, 2, 4 * TM, LANES), BF16),
                   jax.ShapeDtypeStruct((2, n, LANES), BF16),
                   jax.ShapeDtypeStruct((2, n, LANES), BF16),
                   jax.ShapeDtypeStruct((n, 2 * D), BF16)],
        compiler_params=_cp(("arbitrary",), 56),
        name="inproj",
    )(x2, sc1, sh1, g1, w_in_b, cos_t, sin_t, qg, kg, bd128)


def _prep_kernel(p_ref, pprev_ref, pnext_ref, mup_ref, mun_ref, wdec_ref, wicl_ref, gup_ref,
                 w0_ref, a0_ref, kk_ref, ka_ref, rk_ref, bd_ref,
                 r_o, v_o, kk_o, lwf_o, lwb_o, kf_o, kb_o, bf_o, bb_o, go_o, bv_o, *, tpb):
    i = pl.program_id(0)
    ib = i % tpb
    p = p_ref[...]
    row = lax.broadcasted_iota(jnp.int32, (TM, 1), 0)
    hp = jnp.where(ib == 0, 0.0, pprev_ref[7:8, :])
    hn = jnp.where(ib == tpb - 1, 0.0, pnext_ref[0:1, :])
    prev = jnp.where(row == 0, hp, pltpu.roll(p, 1, axis=0))
    nxt = jnp.where(row == TM - 1, hn, pltpu.roll(p, TM - 1, axis=0))
    ps = p + mup_ref[...] * (prev - p) + mun_ref[...] * (nxt - p)

    r = ps[:, 0:RW]
    k = ps[:, RW:2 * RW]
    v = ps[:, 2 * RW:3 * RW]
    dlo = ps[:, 3 * RW:3 * RW + LANES]
    alo = ps[:, 3 * RW + LANES:3 * RW + 2 * LANES]
    glo = ps[:, 3 * RW + 2 * LANES:3 * RW + 3 * LANES]
    bd = bd_ref[...]

    dec = _dot(jnp.tanh(dlo).astype(BF16), wdec_ref[...]) + w0_ref[...]
    icl = _dot(alo.astype(BF16), wicl_ref[...]) + a0_ref[...]
    lw = -DECAY_SCALE * jax.nn.sigmoid(dec)
    a = jax.nn.sigmoid(icl)
    g_out = _dot(jax.nn.sigmoid(glo).astype(BF16), gup_ref[...])

    kk = k * kk_ref[...]
    ssq = _dot((kk * kk).astype(BF16), bd)
    kk = kk * lax.rsqrt(jnp.maximum(ssq, 1e-24))
    ka = ka_ref[...]
    a_f = a[:, 0:RW]
    a_b = a[:, RW:2 * RW]
    k_f = k * (1.0 + (a_f - 1.0) * ka)
    k_b = k * (1.0 + (a_b - 1.0) * ka)
    bonus = _dot((r * (k_f + k_b) * rk_ref[...]).astype(BF16), bd)

    r_o[...] = r
    v_o[...] = v
    kk_o[...] = kk
    lwf_o[...] = lw[:, 0:RW]
    lwb_o[...] = lw[:, RW:2 * RW]
    kf_o[...] = k_f
    kb_o[...] = k_b
    bf_o[...] = a_f * kk
    bb_o[...] = a_b * kk
    go_o[...] = g_out
    bv_o[...] = bonus * v


def _prep(p_rw, mup, mun, wdec, wicl, gup, w0, a0, k_k, k_a, r_k, bd512, n_tok):
    n = p_rw.shape[0]
    tpb = n_tok // TM
    nt8 = n // 8
    row = lambda i: (i, 0)
    c2 = lambda i: (0, 0)
    out = jax.ShapeDtypeStruct((n, RW), F32)
    return pl.pallas_call(
        functools.partial(_prep_kernel, tpb=tpb),
        grid=(n // TM,),
        in_specs=[pl.BlockSpec((TM, RW_COLS), row),
                  pl.BlockSpec((8, RW_COLS), lambda i: (jnp.maximum(i * (TM // 8) - 1, 0), 0)),
                  pl.BlockSpec((8, RW_COLS), lambda i: (jnp.minimum((i + 1) * (TM // 8), nt8 - 1), 0)),
                  pl.BlockSpec((1, RW_COLS), c2), pl.BlockSpec((1, RW_COLS), c2),
                  pl.BlockSpec((LANES, 2 * RW), c2), pl.BlockSpec((LANES, 2 * RW), c2),
                  pl.BlockSpec((LANES, RW), c2),
                  pl.BlockSpec((1, 2 * RW), c2), pl.BlockSpec((1, 2 * RW), c2),
                  pl.BlockSpec((1, RW), c2), pl.BlockSpec((1, RW), c2), pl.BlockSpec((1, RW), c2),
                  pl.BlockSpec((RW, RW), c2)],
        out_specs=[pl.BlockSpec((TM, RW), row)] * 11,
        out_shape=[out] * 11,
        compiler_params=_cp(("arbitrary",)),
        name="rwkv_prep",
    )(p_rw, p_rw, p_rw, mup, mun, wdec, wicl, gup, w0, a0, k_k, k_a, r_k, bd512)


def _bd(x, bdmask):
    return jnp.where(bdmask, jnp.concatenate([x, x, x, x], axis=0), 0.0)


def _wkv_pre(sl, r_ref, v_ref, kk_ref, lw_ref, k_ref, b_ref, tri_ref, rev, cst):
    strict, incl, eye_w, bdmask = cst
    r = r_ref[sl, :]
    v = v_ref[sl, :]
    kk = kk_ref[sl, :]
    lw = lw_ref[sl, :]
    k = k_ref[sl, :]
    b = b_ref[sl, :]
    tri = tri_ref[...]
    l3 = _split3(lw)
    cs = _dot(tri, l3[0]) + _dot(tri, l3[1]) + _dot(tri, l3[2])
    yield
    tot = cs[0:1, :] if rev else cs[CHUNK - 1:CHUNK, :]
    e_neg = jnp.exp(-cs)
    e_end = jnp.exp(tot - cs)
    rq = r * jnp.exp(cs)
    kq = kk * jnp.exp(cs - lw)

    lhs1 = jnp.concatenate([kq, rq], axis=0).astype(BF16)
    rhs1 = jnp.concatenate([_bd(b * e_neg, bdmask), _bd(k * e_neg, bdmask)], axis=0).astype(BF16)
    a_all = _dot_nt(lhs1, rhs1)
    yield
    n_ub = jnp.where(strict, a_all[0:CHUNK, 0:HG], 0.0)
    a_uk = jnp.where(strict, a_all[0:CHUNK, HG:2 * HG], 0.0)
    a_rb = jnp.where(incl, a_all[CHUNK:2 * CHUNK, 0:HG], 0.0).astype(BF16)
    a_rk = jnp.where(incl, a_all[CHUNK:2 * CHUNK, HG:2 * HG], 0.0)

    n_sq = int(math.log2(CHUNK))
    m_j = -n_ub
    t_inv = jnp.where(eye_w, 1.0, 0.0) + m_j
    m_j = _dot(m_j.astype(BF16), _bd(m_j, bdmask).astype(BF16))
    yield
    for j in range(1, n_sq):
        mb = _bd(m_j, bdmask).astype(BF16)
        if j < n_sq - 1:
            both = _dot(jnp.concatenate([t_inv, m_j], axis=0).astype(BF16), mb)
            t_inv = t_inv + both[0:CHUNK]
            m_j = both[CHUNK:2 * CHUNK]
        else:
            t_inv = t_inv + _dot(t_inv.astype(BF16), mb)
        yield

    av = _dot(jnp.concatenate([a_uk, a_rk], axis=0).astype(BF16), _bd(v, bdmask).astype(BF16))
    yield
    tb = t_inv.astype(BF16)
    u0 = -_dot(tb, _bd(av[0:CHUNK], bdmask).astype(BF16))
    pm = _dot(tb, _bd(kq, bdmask).astype(BF16))
    yield
    lhs_t = jnp.concatenate([b * e_end, k * e_end], axis=0).T.astype(BF16)
    wcol = jnp.exp(jnp.broadcast_to(tot, (LANES, HG)).T)
    return dict(u0=u0, pmrq=jnp.concatenate([pm, rq], axis=0).astype(BF16), a_rb=a_rb, yv=av[CHUNK:2 * CHUNK],
                v=v.astype(BF16), lhs_t=lhs_t, wcol=jnp.concatenate([wcol, wcol], axis=1))


def _wkv_seq(sl, pre, st_ref, o_ref, bdmask):
    st = st_ref[...]
    g = _dot(pre["pmrq"], st.astype(BF16))
    yield
    u = pre["u0"] - g[0:CHUNK]
    upd = _dot(pre["lhs_t"], jnp.concatenate([u.astype(BF16), pre["v"]], axis=0))
    yield
    st_ref[...] = st * pre["wcol"] + jnp.where(bdmask, upd, 0.0)
    o_ref[sl, :] = g[CHUNK:2 * CHUNK] + pre["yv"] + _dot(pre["a_rb"], _bd(u, bdmask).astype(BF16))


def _lockstep(gens):
    results = [None] * len(gens)
    live = list(range(len(gens)))
    while live:
        for i in list(live):
            try:
                next(gens[i])
            except StopIteration as stop:
                results[i] = stop.value
                live.remove(i)
    return results


def _wkv_kernel(rf, vf, kkf, lwf, kf, bf, rb, vb, kkb, lwb, kb, bb, trif, trib, of, ob, stf, stb):
    @pl.when(pl.program_id(2) == 0)
    def _():
        stf[...] = jnp.zeros_like(stf)
        stb[...] = jnp.zeros_like(stb)

    row = lax.broadcasted_iota(jnp.int32, (CHUNK, HG), 0)
    s_idx = lax.broadcasted_iota(jnp.int32, (CHUNK, HG), 1) & (HD - 1)
    r2 = lax.broadcasted_iota(jnp.int32, (HG, HG), 0) >> 6
    c2 = lax.broadcasted_iota(jnp.int32, (HG, HG), 1) >> 6
    bdmask = r2 == c2
    eye_w = s_idx == row
    cst_f = (s_idx < row, s_idx <= row, eye_w, bdmask)
    cst_b = (s_idx > row, s_idx >= row, eye_w, bdmask)
    nc = WKV_BLK // CHUNK
    sls = [pl.ds(c * CHUNK, CHUNK) for c in range(nc)]
    pre = _lockstep([_wkv_pre(sl, rf, vf, kkf, lwf, kf, bf, trif, False, cst_f) for sl in sls]
                    + [_wkv_pre(sl, rb, vb, kkb, lwb, kb, bb, trib, True, cst_b) for sl in sls])
    for c in range(nc):
        _lockstep([_wkv_seq(sls[c], pre[c], stf, of, bdmask),
                   _wkv_seq(sls[nc - 1 - c], pre[nc + nc - 1 - c], stb, ob, bdmask)])


def _wkv(r, v, kk, lwf, lwb, kf, kb, bf, bb, trif, trib, bsz, n_tok):
    nb = n_tok // WKV_BLK
    sh3 = lambda a: a.reshape(bsz, n_tok, RW)
    fwd = pl.BlockSpec((None, WKV_BLK, HG), lambda b, g, i: (b, i, g))
    bwd = pl.BlockSpec((None, WKV_BLK, HG), lambda b, g, i: (b, nb - 1 - i, g))
    tri = pl.BlockSpec((CHUNK, CHUNK), lambda b, g, i: (0, 0))
    out = jax.ShapeDtypeStruct((bsz, n_tok, RW), F32)
    of, ob = pl.pallas_call(
        _wkv_kernel,
        grid=(bsz, RW // HG, nb),
        in_specs=[fwd] * 6 + [bwd] * 6 + [tri, tri],
        out_specs=[fwd, bwd],
        out_shape=[out, out],
        scratch_shapes=[pltpu.VMEM((HG, HG), F32), pltpu.VMEM((HG, HG), F32)],
        compiler_params=_cp(("arbitrary", "arbitrary", "arbitrary")),
        name="wkv",
    )(sh3(r), sh3(v), sh3(kk), sh3(lwf), sh3(kf), sh3(bf),
      sh3(r), sh3(v), sh3(kk), sh3(lwb), sh3(kb), sh3(bb), trif, trib)
    return of.reshape(-1, RW), ob.reshape(-1, RW)


def _attn_kernel(q_ref, k_ref, v_ref, o_ref, m_scr, acc_scr, *, n_kb, tk):
    q = q_ref[...]
    m_scr[...] = jnp.full_like(m_scr, NEG_BIG)
    acc_scr[...] = jnp.zeros_like(acc_scr)

    def body(kb, carry):
        sl = pl.ds(pl.multiple_of(kb * tk, tk), tk)
        s = _dot_nt(q, k_ref[sl, :])
        m_prev = m_scr[...]
        m_new = jnp.maximum(m_prev, jnp.max(s, axis=1, keepdims=True))
        alpha = jnp.exp2(m_prev - m_new)
        p = jnp.exp2(s - jnp.tile(m_new, (1, tk // LANES)))
        acc_scr[...] = alpha * acc_scr[...] + _dot(p.astype(BF16), v_ref[sl, :])
        m_scr[...] = m_new
        return carry

    lax.fori_loop(0, n_kb, body, 0, unroll=2 if n_kb % 2 == 0 else 1)
    acc = acc_scr[...]
    o = acc / acc[:, HD:HD + 1]
    low = lax.broadcasted_iota(jnp.int32, (TM, LANES), 1) < HD
    for pr in range(2):
        oa = o[(2 * pr) * TM:(2 * pr + 1) * TM, :]
        ob = o[(2 * pr + 1) * TM:(2 * pr + 2) * TM, :]
        o_ref[:, pr * LANES:(pr + 1) * LANES] = jnp.where(low, oa, pltpu.roll(ob, HD, axis=1)).astype(BF16)


def _attention(q_st, k, v, bsz, n_tok):
    nq = n_tok // TM
    tk = min(ATT_TK, n_tok)
    q5 = q_st.reshape(bsz, nq, 2, 4 * TM, LANES)
    k4 = k.reshape(2, bsz, n_tok, LANES)
    v4 = v.reshape(2, bsz, n_tok, LANES)
    kv = pl.BlockSpec((None, None, n_tok, LANES), lambda b, g, i: (g, b, 0, 0))
    o = pl.pallas_call(
        functools.partial(_attn_kernel, n_kb=n_tok // tk, tk=tk),
        grid=(bsz, 2, nq),
        in_specs=[pl.BlockSpec((None, None, None, 4 * TM, LANES), lambda b, g, i: (b, i, g, 0, 0)), kv, kv],
        out_specs=pl.BlockSpec((None, TM, 2 * LANES), lambda b, g, i: (b, i, g)),
        out_shape=jax.ShapeDtypeStruct((bsz, n_tok, RW), BF16),
        scratch_shapes=[pltpu.VMEM((4 * TM, LANES), F32)] * 2,
        compiler_params=_cp(("arbitrary", "arbitrary", "arbitrary"), 56),
        name="attention",
    )(q5, k4, v4)
    return o.reshape(-1, RW)


def _merge_kernel(of_ref, ob_ref, bv_ref, go_ref, oat_ref, gate_ref, x_ref, gt1_ref, sc2_ref, sh2_ref,
                  lnw_ref, lnb_ref, g2_ref, wbr_ref, wba_ref, wout_ref, wrh_ref, wrl_ref, br_ref,
                  bd_ref, tri_ref, x1_ref, h2_ref, route_ref, cnt_ref, carry):
    @pl.when(pl.program_id(0) == 0)
    def _():
        carry[...] = jnp.zeros_like(carry)

    bd = bd_ref[...]
    o = of_ref[...] + ob_ref[...]
    oh, om, _ = _split3(o)
    mu = (_dot(oh, bd) + _dot(om, bd)) * (1.0 / HD)
    d = o - mu
    dh, dm, _ = _split3(d * d)
    var = (_dot(dh, bd) + _dot(dm, bd)) * (1.0 / HD)
    on = d * lax.rsqrt(var + GN_EPS) * lnw_ref[...] + lnb_ref[...]
    o_rw = (on + bv_ref[...]) * go_ref[...]
    br = _dot(o_rw.astype(BF16), wbr_ref[...])
    ba = _dot(oat_ref[...], wba_ref[...])
    gates = gate_ref[...]
    merged = gates[:, 0:D].astype(F32) * br + gates[:, D:2 * D].astype(F32) * ba
    x1 = x_ref[...] + gt1_ref[...] * _dot(merged.astype(BF16), wout_ref[...])
    x1_ref[...] = x1
    ms = jnp.mean(x1 * x1, axis=-1, keepdims=True)
    h2 = x1 * lax.rsqrt(ms + EPS) * g2_ref[...]
    h2 = h2 * (1.0 + sc2_ref[...]) + sh2_ref[...]
    _store_rows(h2_ref, h2, TM)

    hh, hm, hl = _split3(h2)
    wh = wrh_ref[...]
    wl = wrl_ref[...]
    logits = _dot(hh, wh) + (_dot(hh, wl) + _dot(hm, wh)) + (_dot(hm, wl) + _dot(hl, wh)) + br_ref[...]

    lane = lax.broadcasted_iota(jnp.int32, (TM, LANES), 1).astype(F32)
    cur = logits
    vals, idxs = [], []
    for _ in range(TOP_K):
        m = jnp.max(cur, axis=1, keepdims=True)
        ix = jnp.min(jnp.where(cur == m, lane, float(LANES)), axis=1, keepdims=True)
        vals.append(m)
        idxs.append(ix)
        cur = jnp.where(lane == ix, -jnp.inf, cur)
    es = [jnp.exp(vv - vals[0]) for vv in vals]
    den = es[0] + es[1] + es[2] + es[3]
    onehot = jnp.zeros((TM, LANES), F32)
    for ix in idxs:
        onehot = onehot + jnp.where(lane == ix, 1.0, 0.0)
    cnt = _dot(tri_ref[...], onehot.astype(BF16)) + carry[...]
    route = jnp.zeros((TM, LANES), F32)
    for j in range(TOP_K):
        rank = jnp.sum(jnp.where(lane == idxs[j], cnt, 0.0), axis=1, keepdims=True)
        route = jnp.where(lane == float(j), idxs[j], route)
        route = jnp.where(lane == float(TOP_K + j), rank, route)
        route = jnp.where(lane == float(2 * TOP_K + j), es[j] / den, route)
    route_ref[...] = route
    carry[...] = carry[...] + jnp.sum(onehot, axis=0, keepdims=True)
    cnt_ref[...] = jnp.broadcast_to(carry[...], (8, LANES))


def _merge(of, ob, bv, go, oat, gates, x2, gt1, sc2, sh2, lnw, lnb, g2, wbr, wba, wout, wrh, wrl, brt,
           bd512, tri_tm, n_tok):
    n = x2.shape[0]
    tpb = n_tok // TM
    row = lambda i: (i, 0)
    per_b = lambda i: (i // tpb, 0, 0)
    c2 = lambda i: (0, 0)
    rw = pl.BlockSpec((TM, RW), row)
    full = lambda a: pl.BlockSpec(a.shape, c2)
    mod = pl.BlockSpec((None, 1, D), per_b)
    return pl.pallas_call(
        _merge_kernel,
        grid=(n // TM,),
        in_specs=[rw, rw, rw, rw, pl.BlockSpec((TM, RW), row), pl.BlockSpec((TM, 2 * D), row),
                  pl.BlockSpec((TM, D), row), mod, mod, mod,
                  full(lnw), full(lnb), full(g2), full(wbr), full(wba), full(wout), full(wrh), full(wrl),
                  full(brt), full(bd512), full(tri_tm)],
        out_specs=[pl.BlockSpec((TM, D), row), pl.BlockSpec((TM * ROW_SUB, LANES), row),
                   pl.BlockSpec((TM, LANES), row),
                   pl.BlockSpec((8, LANES), c2)],
        out_shape=[jax.ShapeDtypeStruct((n, D), F32), jax.ShapeDtypeStruct((n * ROW_SUB, LANES), F32),
                   jax.ShapeDtypeStruct((n, LANES), F32), jax.ShapeDtypeStruct((8, LANES), F32)],
        scratch_shapes=[pltpu.VMEM((1, LANES), F32)],
        compiler_params=_cp(("arbitrary",)),
        name="merge",
    )(of, ob, bv, go, oat, gates, x2, gt1, sc2, sh2, lnw, lnb, g2, wbr, wba, wout, wrh, wrl, brt,
      bd512, tri_tm)


def _slots_kernel(route_ref, pstart_ref, o_ref):
    route = route_ref[...]
    ps = pstart_ref[...]
    lane = lax.broadcasted_iota(jnp.int32, route.shape, 1).astype(F32)
    out = jnp.zeros(route.shape, F32)
    for j in range(TOP_K):
        off = jnp.sum(jnp.where(lane == route[:, j:j + 1], ps, 0.0), axis=1, keepdims=True)
        out = jnp.where(lane == float(j), off + route[:, TOP_K + j:TOP_K + j + 1], out)
    o_ref[...] = out.astype(jnp.int32)


def _slots(route, pstart_row):
    n = route.shape[0]
    return pl.pallas_call(
        _slots_kernel,
        grid=(n // SLOT_TM,),
        in_specs=[pl.BlockSpec((SLOT_TM, LANES), lambda i: (i, 0)), pl.BlockSpec((1, LANES), lambda i: (0, 0))],
        out_specs=pl.BlockSpec((SLOT_TM, LANES), lambda i: (i, 0)),
        out_shape=jax.ShapeDtypeStruct((n, LANES), jnp.int32),
        compiler_params=_cp(("arbitrary",)),
        name="moe_slots",
    )(route, pstart_row)


def _dispatch_kernel(slot_ref, pend_ref, npad_ref, h2_ref, xs_hbm, zbuf, sem):
    @pl.when(pl.program_id(0) == 0)
    def _():
        zbuf[...] = jnp.zeros_like(zbuf)

        def fill_tail(blk, carry):
            first = pl.multiple_of(blk * (MOE_BM * ROW_SUB), MOE_BM * ROW_SUB)
            fill = pltpu.make_async_copy(zbuf, xs_hbm.at[pl.ds(first, MOE_BM * ROW_SUB)], sem)
            fill.start()
            fill.wait()
            return carry

        lax.fori_loop(pend_ref[N_EXPERTS - 1] // MOE_BM, xs_hbm.shape[0] // (MOE_BM * ROW_SUB), fill_tail, 0)
        for e in range(N_EXPERTS):
            @pl.when(npad_ref[e] > 0)
            def _():
                last = pl.multiple_of((pend_ref[e] - MOE_BM) * ROW_SUB, MOE_BM * ROW_SUB)
                fill = pltpu.make_async_copy(zbuf, xs_hbm.at[pl.ds(last, MOE_BM * ROW_SUB)], sem)
                fill.start()
                fill.wait()

    def body(t, carry):
        for j in range(TOP_K):
            pltpu.make_async_copy(h2_ref.at[_tile(t)], xs_hbm.at[_tile(slot_ref[t * TOP_K + j])],
                                  sem).start(priority=j % 2)
        return carry

    lax.fori_loop(0, TM, body, 0, unroll=DMA_UNROLL)
    for _ in range(TOP_K):
        pltpu.make_async_copy(h2_ref, xs_hbm.at[pl.ds(0, TM * ROW_SUB)], sem).wait()


def _dispatch(slot_flat, pad_end, padded, h2, cap):
    n = h2.shape[0] // ROW_SUB
    return pl.pallas_call(
        _dispatch_kernel,
        grid=(n // TM,),
        in_specs=[pl.BlockSpec((TM * TOP_K,), lambda i: (i,), memory_space=pltpu.SMEM),
                  pl.BlockSpec(memory_space=pltpu.SMEM),
                  pl.BlockSpec(memory_space=pltpu.SMEM),
                  pl.BlockSpec((TM * ROW_SUB, LANES), lambda i: (i, 0))],
        out_specs=pl.BlockSpec(memory_space=pl.ANY),
        out_shape=jax.ShapeDtypeStruct((cap * ROW_SUB, LANES), F32),
        scratch_shapes=[pltpu.VMEM((MOE_BM * ROW_SUB, LANES), F32), pltpu.SemaphoreType.DMA(())],
        compiler_params=_cp(("arbitrary",)),
        name="moe_dispatch",
    )(slot_flat, pad_end, padded, h2)


def _expert_kernel(bexp_ref, nused_ref, xs_ref, wgu_ref, bgu_ref, wd_ref, bdn_ref, ys_ref):
    i = pl.program_id(0)

    @pl.when(i < nused_ref[0])
    def _():
        gu = _dot(_load_rows(xs_ref, MOE_BM).astype(BF16), wgu_ref[...]) + bgu_ref[...]
        g_lin = jnp.minimum(gu[:, 0:D_FF], SWIGLU_LIMIT)
        u_lin = jnp.clip(gu[:, D_FF:2 * D_FF], -SWIGLU_LIMIT, SWIGLU_LIMIT)
        act = (u_lin + 1.0) * (g_lin * jax.nn.sigmoid(SWIGLU_ALPHA * g_lin))
        _store_rows(ys_ref, _dot(act.astype(BF16), wd_ref[...]) + bdn_ref[...], MOE_BM)

    @pl.when(i >= nused_ref[0])
    def _():
        ys_ref[...] = jnp.zeros_like(ys_ref)


def _experts(blk_exp, nused, xs, wgu_b, bgu, wd_b, bdn):
    cap = xs.shape[0] // ROW_SUB
    grid_spec = pltpu.PrefetchScalarGridSpec(
        num_scalar_prefetch=2,
        grid=(cap // MOE_BM,),
        in_specs=[pl.BlockSpec((MOE_BM * ROW_SUB, LANES), lambda i, be, nu: (jnp.minimum(i, nu[0] - 1), 0)),
                  pl.BlockSpec((None, D, 2 * D_FF), lambda i, be, nu: (be[i], 0, 0)),
                  pl.BlockSpec((None, 1, 2 * D_FF), lambda i, be, nu: (be[i], 0, 0)),
                  pl.BlockSpec((None, D_FF, D), lambda i, be, nu: (be[i], 0, 0)),
                  pl.BlockSpec((None, 1, D), lambda i, be, nu: (be[i], 0, 0))],
        out_specs=pl.BlockSpec((MOE_BM * ROW_SUB, LANES), lambda i, be, nu: (i, 0)),
    )
    return pl.pallas_call(
        _expert_kernel,
        grid_spec=grid_spec,
        out_shape=jax.ShapeDtypeStruct(xs.shape, F32),
        compiler_params=_cp(("arbitrary",), 56),
        name="moe_experts",
    )(blk_exp, nused, xs, wgu_b, bgu, wd_b, bdn)


def _combine_kernel(slot_ref, ys_hbm, route_ref, x1_ref, gt2_ref, gf_ref, o_ref, buf, sem):
    def body(t, carry):
        for j in range(TOP_K):
            pltpu.make_async_copy(ys_hbm.at[_tile(slot_ref[t * TOP_K + j])], buf.at[j, _tile(t)],
                                  sem).start(priority=j % 2)
        return carry

    lax.fori_loop(0, TM, body, 0, unroll=DMA_UNROLL)
    for j in range(TOP_K):
        pltpu.make_async_copy(ys_hbm.at[pl.ds(0, TM * ROW_SUB)], buf.at[j], sem).wait()
    route = route_ref[...]
    y = jnp.zeros((TM, D), F32)
    for j in range(TOP_K):
        y = y + route[:, 2 * TOP_K + j:2 * TOP_K + j + 1] * _load_rows(buf, TM, lead=(j,))
    x = x1_ref[...] + gt2_ref[...] * y
    ms = jnp.mean(x * x, axis=-1, keepdims=True)
    o_ref[...] = x * lax.rsqrt(ms + EPS) * gf_ref[...]


def _combine(slot_flat, ys, route, x1, gt2, gf, n_tok):
    n = x1.shape[0]
    tpb = n_tok // TM
    row = lambda i: (i, 0)
    return pl.pallas_call(
        _combine_kernel,
        grid=(n // TM,),
        in_specs=[pl.BlockSpec((TM * TOP_K,), lambda i: (i,), memory_space=pltpu.SMEM),
                  pl.BlockSpec(memory_space=pl.ANY),
                  pl.BlockSpec((TM, LANES), row),
                  pl.BlockSpec((TM, D), row),
                  pl.BlockSpec((None, 1, D), lambda i: (i // tpb, 0, 0)),
                  pl.BlockSpec((1, D), lambda i: (0, 0))],
        out_specs=pl.BlockSpec((TM, D), row),
        out_shape=jax.ShapeDtypeStruct((n, D), F32),
        scratch_shapes=[pltpu.VMEM((TOP_K, TM * ROW_SUB, LANES), F32), pltpu.SemaphoreType.DMA(())],
        compiler_params=_cp(("arbitrary",)),
        name="moe_combine",
    )(slot_flat, ys, route, x1, gt2, gf)


def _rope_tables(n_tok):
    t = jnp.arange(n_tok)
    row = (t // GRID_W).astype(F32)
    col = (t % GRID_W).astype(F32)
    nf = HD // 4
    freqs = ROPE_THETA ** (-jnp.arange(nf, dtype=F32) / nf)
    ang = jnp.concatenate([row[:, None] * freqs, col[:, None] * freqs], axis=-1)
    cos = jnp.repeat(jnp.cos(ang), 2, axis=-1)
    sin = jnp.repeat(jnp.sin(ang), 2, axis=-1)
    sign = jnp.tile(jnp.array([-1.0, 1.0], F32), HD // 2)
    return jnp.tile(cos, (1, 2)), jnp.tile(sin * sign, (1, 2))


def _block_diag2(a, b):
    z = jnp.zeros_like(a)
    return jnp.concatenate([jnp.concatenate([a, z], axis=1), jnp.concatenate([z, b], axis=1)], axis=0)


def _prepare_weights(w):
    f = {}
    f["w_in"] = w["w_in"][0].astype(BF16)
    f["g1"] = w["norm1_g"][0].reshape(1, D)
    f["g2"] = w["norm2_g"][0].reshape(1, D)
    f["qg"] = jnp.tile(w["q_norm_g"][0], 2).reshape(1, LANES)
    f["kg"] = jnp.tile(w["k_norm_g"][0], 2).reshape(1, LANES)
    ones = jnp.ones((HD, HD), F32)
    f["bd128"] = jnp.kron(jnp.eye(2, dtype=F32), ones).astype(BF16)
    f["bd512"] = jnp.kron(jnp.eye(NH, dtype=F32), ones).astype(BF16)
    f["mup"] = w["mu_prev"][0].reshape(1, RW_COLS)
    f["mun"] = w["mu_next"][0].reshape(1, RW_COLS)
    f["wdec"] = _block_diag2(w["wb_f"][0], w["wb_b"][0]).astype(BF16)
    f["wicl"] = _block_diag2(w["ab_f"][0], w["ab_b"][0]).astype(BF16)
    f["gup"] = w["g_up"][0].astype(BF16)
    f["w0"] = jnp.concatenate([w["w0_f"][0], w["w0_b"][0]]).reshape(1, 2 * RW)
    f["a0"] = jnp.concatenate([w["a0_f"][0], w["a0_b"][0]]).reshape(1, 2 * RW)
    f["k_k"] = w["k_k"][0].reshape(1, RW)
    f["k_a"] = w["k_a"][0].reshape(1, RW)
    f["r_k"] = w["r_k"][0].reshape(1, RW)
    ti = jnp.arange(CHUNK)
    f["trif"] = (ti[None, :] <= ti[:, None]).astype(BF16)
    f["trib"] = (ti[None, :] >= ti[:, None]).astype(BF16)
    tm = jnp.arange(TM)
    f["tri_tm"] = (tm[None, :] < tm[:, None]).astype(BF16)
    f["lnw"] = w["lnx_w"][0].reshape(1, RW)
    f["lnb"] = w["lnx_b"][0].reshape(1, RW)
    f["wbr"] = w["w_br_rwkv"][0].astype(BF16)
    f["wba"] = w["w_br_attn"][0].astype(BF16)
    f["wout"] = w["w_out"][0].astype(BF16)
    wr = jnp.pad(w["w_router"][0], ((0, 0), (0, LANES - N_EXPERTS)))
    wrh = wr.astype(BF16)
    f["wrh"] = wrh
    f["wrl"] = (wr - wrh.astype(F32)).astype(BF16)
    f["br"] = jnp.pad(w["b_router"][0], (0, LANES - N_EXPERTS), constant_values=NEG_BIG).reshape(1, LANES)
    f["wgu"] = w["w_gu"][0].astype(BF16)
    f["bgu"] = w["b_gu"][0].reshape(N_EXPERTS, 1, 2 * D_FF)
    f["wd"] = w["w_down"][0].astype(BF16)
    f["bdn"] = w["b_down"][0].reshape(N_EXPERTS, 1, D)
    f["gf"] = w["normf_g"].reshape(1, D)
    return f


def _run(x, c, w, f):
    bsz, n_tok, _ = x.shape
    n = bsz * n_tok
    x2 = x.reshape(n, D)
    c8 = jnp.pad(c, ((0, 8 - bsz), (0, 0)))
    mod = _ada(c8, w["w_ada"][0], w["b_ada"][0])[:bsz]
    sh1, sc1, gt1, sh2, sc2, gt2 = [m.reshape(bsz, 1, D) for m in jnp.split(mod, 6, axis=-1)]

    cos_t, sin_t = _rope_tables(n_tok)
    p_rw, q_st, k_att, v_att, gates = _inproj(x2, sc1, sh1, f["g1"], f["w_in"], cos_t, sin_t,
                                              f["qg"], f["kg"], f["bd128"], n_tok)
    r, v, kk, lwf, lwb, kf, kb, bf, bb, go, bv = _prep(
        p_rw, f["mup"], f["mun"], f["wdec"], f["wicl"], f["gup"], f["w0"], f["a0"],
        f["k_k"], f["k_a"], f["r_k"], f["bd512"], n_tok)
    of, ob = _wkv(r, v, kk, lwf, lwb, kf, kb, bf, bb, f["trif"], f["trib"], bsz, n_tok)
    o_at = _attention(q_st, k_att, v_att, bsz, n_tok)
    x1, h2, route, cnt = _merge(of, ob, bv, go, o_at, gates, x2, gt1, sc2, sh2, f["lnw"], f["lnb"], f["g2"],
                                f["wbr"], f["wba"], f["wout"], f["wrh"], f["wrl"], f["br"],
                                f["bd512"], f["tri_tm"], n_tok)

    counts = cnt[0, :N_EXPERTS].astype(jnp.int32)
    padded = ((counts + MOE_BM - 1) // MOE_BM * MOE_BM).astype(jnp.int32)
    pad_end = jnp.cumsum(padded).astype(jnp.int32)
    pstart_row = jnp.pad((pad_end - padded).astype(F32), (0, LANES - N_EXPERTS)).reshape(1, LANES)
    nk = n * TOP_K
    nblk = -(-nk // MOE_BM) + N_EXPERTS
    blk_first = (jnp.arange(nblk) * MOE_BM)[:, None]
    blk_exp = jnp.minimum(jnp.sum(pad_end[None, :] <= blk_first, axis=1), N_EXPERTS - 1).astype(jnp.int32)
    nused = (pad_end[-1:] // MOE_BM).astype(jnp.int32)

    slot_flat = _slots(route, pstart_row)[:, 0:TOP_K].reshape(nk)
    xs = _dispatch(slot_flat, pad_end, padded, h2, nblk * MOE_BM)
    ys = _experts(blk_exp, nused, xs, f["wgu"], f["bgu"], f["wd"], f["bdn"])
    y = _combine(slot_flat, ys, route, x1, gt2, f["gf"], n_tok)
    return y.reshape(bsz, n_tok, D)


def kernel(x_prompt, x_sample, c_prompt, c_sample, norm1_g, norm2_g, w_ada, b_ada, w_in, mu_prev, mu_next, w0_f, w0_b, wb_f, wb_b, a0_f, a0_b, ab_f, ab_b, k_k, k_a, r_k, g_up, lnx_w, lnx_b, q_norm_g, k_norm_g, w_br_rwkv, w_br_attn, w_out, w_router, b_router, w_gu, b_gu, w_down, b_down, normf_g):
    w = dict(norm1_g=norm1_g, norm2_g=norm2_g, w_ada=w_ada, b_ada=b_ada, w_in=w_in, mu_prev=mu_prev,
             mu_next=mu_next, w0_f=w0_f, w0_b=w0_b, wb_f=wb_f, wb_b=wb_b, a0_f=a0_f, a0_b=a0_b, ab_f=ab_f,
             ab_b=ab_b, k_k=k_k, k_a=k_a, r_k=r_k, g_up=g_up, lnx_w=lnx_w, lnx_b=lnx_b, q_norm_g=q_norm_g,
             k_norm_g=k_norm_g, w_br_rwkv=w_br_rwkv, w_br_attn=w_br_attn, w_out=w_out, w_router=w_router,
             b_router=b_router, w_gu=w_gu, b_gu=b_gu, w_down=w_down, b_down=b_down, normf_g=normf_g)
    f = _prepare_weights(w)
    return (_run(x_prompt, c_prompt, w, f), _run(x_sample, c_sample, w, f))
```

```python
import functools
import math

import jax
import jax.numpy as jnp
from jax import lax
from jax.experimental import pallas as pl
from jax.experimental.pallas import tpu as pltpu

F32 = jnp.float32
BF16 = jnp.bfloat16
ATT_DT = jnp.float8_e4m3fn
PV_DT = jnp.bfloat16

D = 1024
GRID_W = 64
NH = 8
HD = 64
RW = NH * HD
RW_COLS = 1920
Q_OFF, K_OFF, V_OFF, G_OFF, IN_COLS = 1920, 2432, 2560, 2688, 4736
DECAY_SCALE = math.exp(-0.5)
GN_EPS = 64e-5
EPS = 1e-6
ROPE_THETA = 10000.0
N_EXPERTS = 32
TOP_K = 4
D_FF = 1024
SWIGLU_LIMIT = 7.0
SWIGLU_ALPHA = 1.702

TM = 256
CHUNK = 64
WKV_BLK = 256
HG = 256
ATT_TK = 2048
MOE_BM = 512
SLOT_TM = 1024
DMA_UNROLL = 4
ROW_SUB = 8
LANES = 128
NEG_BIG = -1e30


def _cp(sem, vmem_mb=48):
    return pltpu.CompilerParams(dimension_semantics=sem, vmem_limit_bytes=vmem_mb << 20)


def _split3(x):
    h = x.astype(BF16)
    r1 = x - h.astype(F32)
    m = r1.astype(BF16)
    lo = (r1 - m.astype(F32)).astype(BF16)
    return h, m, lo


def _dot(a, b):
    return jnp.dot(a, b, preferred_element_type=F32)


def _dot_nt(a, b):
    return lax.dot_general(a, b, (((1,), (1,)), ((), ())), preferred_element_type=F32)


def _tile(r):
    return pl.ds(pl.multiple_of(r * ROW_SUB, ROW_SUB), ROW_SUB)


def _load_rows(ref, n_rows, lead=()):
    return jnp.concatenate([ref[lead + (pl.ds(sub, n_rows, stride=ROW_SUB), slice(None))]
                            for sub in range(ROW_SUB)], axis=1)


def _store_rows(ref, val, n_rows):
    for sub in range(ROW_SUB):
        ref[pl.ds(sub, n_rows, stride=ROW_SUB), :] = val[:, sub * LANES:(sub + 1) * LANES]


def _ada_kernel(c_ref, w_ref, b_ref, o_ref):
    c = c_ref[...]
    s = c * jax.nn.sigmoid(c)
    sh, sm, sl = _split3(s)
    wh, wm, wl = _split3(w_ref[...])
    acc = _dot(sh, wh) + (_dot(sh, wm) + _dot(sm, wh)) + (_dot(sm, wm) + _dot(sh, wl) + _dot(sl, wh))
    o_ref[...] = acc + b_ref[...]


def _ada(c8, w_ada, b_ada):
    n_mod = w_ada.shape[1] // D
    return pl.pallas_call(
        _ada_kernel,
        grid=(n_mod,),
        in_specs=[pl.BlockSpec((8, D), lambda j: (0, 0)),
                  pl.BlockSpec((D, D), lambda j: (0, j)),
                  pl.BlockSpec((1, D), lambda j: (0, j))],
        out_specs=pl.BlockSpec((8, D), lambda j: (0, j)),
        out_shape=jax.ShapeDtypeStruct((8, n_mod * D), F32),
        compiler_params=_cp(("arbitrary",)),
        name="ada",
    )(c8, w_ada, b_ada.reshape(1, -1))


def _rope(x, c, s, even):
    swap = jnp.where(even, pltpu.roll(x, LANES - 1, axis=1), pltpu.roll(x, 1, axis=1))
    return x * c + swap * s


def _inproj_kernel(x_ref, sc_ref, sh_ref, g_ref, w_ref, cos_ref, sin_ref, qg_ref, kg_ref, bd_ref,
                   prw_ref, q_ref, k_ref, v_ref, gate_ref):
    x = x_ref[...]
    ms = jnp.mean(x * x, axis=-1, keepdims=True)
    h = x * lax.rsqrt(ms + EPS) * g_ref[...]
    h = h * (1.0 + sc_ref[...]) + sh_ref[...]
    hb = h.astype(BF16)
    prw_ref[...] = _dot(hb, w_ref[:, 0:RW_COLS])
    gate_ref[...] = jax.nn.sigmoid(_dot(hb, w_ref[:, G_OFF:IN_COLS])).astype(BF16)

    cos = cos_ref[...]
    sin = sin_ref[...]
    bd = bd_ref[...]
    lane = lax.broadcasted_iota(jnp.int32, (TM, LANES), 1)
    even = (lane & 1) == 0
    low = lane < HD

    def norm_rope(slab, gain):
        ssq = _dot((slab * slab).astype(BF16), bd)
        return _rope(slab * lax.rsqrt(ssq * (1.0 / HD) + EPS) * gain, cos, sin, even)

    kr = norm_rope(_dot(hb, w_ref[:, K_OFF:V_OFF]), kg_ref[...])
    k_ref[0] = jnp.where(low, kr, 0.0).astype(ATT_DT)
    k_ref[1] = jnp.where(low, pltpu.roll(kr, HD, axis=1), 0.0).astype(ATT_DT)
    vf = _dot(hb, w_ref[:, V_OFF:G_OFF])
    one_col = jnp.where(lane == HD, 1.0, 0.0)
    v_ref[0] = jnp.where(low, vf, one_col).astype(PV_DT)
    v_ref[1] = jnp.where(low, pltpu.roll(vf, HD, axis=1), one_col).astype(PV_DT)

    qf = _dot(hb, w_ref[:, Q_OFF:K_OFF])
    scale = HD ** -0.5 * math.log2(math.e)
    for j in range(NH // 2):
        slab = norm_rope(qf[:, j * LANES:(j + 1) * LANES], qg_ref[...]) * scale
        g, ha = (2 * j) // 4, (2 * j) % 4
        q_ref[g, ha * TM:(ha + 1) * TM, :] = jnp.where(low, slab, 0.0).astype(ATT_DT)
        q_ref[g, (ha + 1) * TM:(ha + 2) * TM, :] = jnp.where(low, pltpu.roll(slab, HD, axis=1), 0.0).astype(ATT_DT)


def _inproj(x2, sc1, sh1, g1, w_in_b, cos_t, sin_t, qg, kg, bd128, n_tok):
    n = x2.shape[0]
    tpb = n_tok // TM
    row = lambda i: (i, 0)
    per_b = lambda i: (i // tpb, 0, 0)
    const2 = lambda i: (0, 0)
    tab = lambda i: (i % tpb, 0)
    return pl.pallas_call(
        _inproj_kernel,
        grid=(n // TM,),
        in_specs=[pl.BlockSpec((TM, D), row),
                  pl.BlockSpec((None, 1, D), per_b),
                  pl.BlockSpec((None, 1, D), per_b),
                  pl.BlockSpec((1, D), const2),
                  pl.BlockSpec((D, IN_COLS), const2),
                  pl.BlockSpec((TM, LANES), tab),
                  pl.BlockSpec((TM, LANES), tab),
                  pl.BlockSpec((1, LANES), const2),
                  pl.BlockSpec((1, LANES), const2),
                  pl.BlockSpec((LANES, LANES), const2)],
        out_specs=[pl.BlockSpec((TM, RW_COLS), row),
                   pl.BlockSpec((None, 2, 4 * TM, LANES), lambda i: (i, 0, 0, 0)),
                   pl.BlockSpec((2, TM, LANES), lambda i: (0, i, 0)),
                   pl.BlockSpec((2, TM, LANES), lambda i: (0, i, 0)),
                   pl.BlockSpec((TM, 2 * D), row)],
        out_shape=[jax.ShapeDtypeStruct((n, RW_COLS), F32),
                   jax.ShapeDtypeStruct((n // TM, 2, 4 * TM, LANES), ATT_DT),
                   jax.ShapeDtypeStruct((2, n, LANES), ATT_DT),
                   jax.ShapeDtypeStruct((2, n, LANES), PV_DT),
                   jax.ShapeDtypeStruct((n, 2 * D), BF16)],
        compiler_params=_cp(("arbitrary",), 56),
        name="inproj",
    )(x2, sc1, sh1, g1, w_in_b, cos_t, sin_t, qg, kg, bd128)


def _prep_kernel(p_ref, pprev_ref, pnext_ref, mup_ref, mun_ref, wdec_ref, wicl_ref, gup_ref,
                 w0_ref, a0_ref, kk_ref, ka_ref, rk_ref, bd_ref,
                 r_o, v_o, kk_o, lwf_o, lwb_o, kf_o, kb_o, bf_o, bb_o, go_o, bv_o, *, tpb):
    i = pl.program_id(0)
    ib = i % tpb
    p = p_ref[...]
    row = lax.broadcasted_iota(jnp.int32, (TM, 1), 0)
    hp = jnp.where(ib == 0, 0.0, pprev_ref[7:8, :])
    hn = jnp.where(ib == tpb - 1, 0.0, pnext_ref[0:1, :])
    prev = jnp.where(row == 0, hp, pltpu.roll(p, 1, axis=0))
    nxt = jnp.where(row == TM - 1, hn, pltpu.roll(p, TM - 1, axis=0))
    ps = p + mup_ref[...] * (prev - p) + mun_ref[...] * (nxt - p)

    r = ps[:, 0:RW]
    k = ps[:, RW:2 * RW]
    v = ps[:, 2 * RW:3 * RW]
    dlo = ps[:, 3 * RW:3 * RW + LANES]
    alo = ps[:, 3 * RW + LANES:3 * RW + 2 * LANES]
    glo = ps[:, 3 * RW + 2 * LANES:3 * RW + 3 * LANES]
    bd = bd_ref[...]

    dec = _dot(jnp.tanh(dlo).astype(BF16), wdec_ref[...]) + w0_ref[...]
    icl = _dot(alo.astype(BF16), wicl_ref[...]) + a0_ref[...]
    lw = -DECAY_SCALE * jax.nn.sigmoid(dec)
    a = jax.nn.sigmoid(icl)
    g_out = _dot(jax.nn.sigmoid(glo).astype(BF16), gup_ref[...])

    kk = k * kk_ref[...]
    ssq = _dot((kk * kk).astype(BF16), bd)
    kk = kk * lax.rsqrt(jnp.maximum(ssq, 1e-24))
    ka = ka_ref[...]
    a_f = a[:, 0:RW]
    a_b = a[:, RW:2 * RW]
    k_f = k * (1.0 + (a_f - 1.0) * ka)
    k_b = k * (1.0 + (a_b - 1.0) * ka)
    bonus = _dot((r * (k_f + k_b) * rk_ref[...]).astype(BF16), bd)

    r_o[...] = r
    v_o[...] = v
    kk_o[...] = kk
    lwf_o[...] = lw[:, 0:RW]
    lwb_o[...] = lw[:, RW:2 * RW]
    kf_o[...] = k_f
    kb_o[...] = k_b
    bf_o[...] = a_f * kk
    bb_o[...] = a_b * kk
    go_o[...] = g_out
    bv_o[...] = bonus * v


def _prep(p_rw, mup, mun, wdec, wicl, gup, w0, a0, k_k, k_a, r_k, bd512, n_tok):
    n = p_rw.shape[0]
    tpb = n_tok // TM
    nt8 = n // 8
    row = lambda i: (i, 0)
    c2 = lambda i: (0, 0)
    out = jax.ShapeDtypeStruct((n, RW), F32)
    return pl.pallas_call(
        functools.partial(_prep_kernel, tpb=tpb),
        grid=(n // TM,),
        in_specs=[pl.BlockSpec((TM, RW_COLS), row),
                  pl.BlockSpec((8, RW_COLS), lambda i: (jnp.maximum(i * (TM // 8) - 1, 0), 0)),
                  pl.BlockSpec((8, RW_COLS), lambda i: (jnp.minimum((i + 1) * (TM // 8), nt8 - 1), 0)),
                  pl.BlockSpec((1, RW_COLS), c2), pl.BlockSpec((1, RW_COLS), c2),
                  pl.BlockSpec((LANES, 2 * RW), c2), pl.BlockSpec((LANES, 2 * RW), c2),
                  pl.BlockSpec((LANES, RW), c2),
                  pl.BlockSpec((1, 2 * RW), c2), pl.BlockSpec((1, 2 * RW), c2),
                  pl.BlockSpec((1, RW), c2), pl.BlockSpec((1, RW), c2), pl.BlockSpec((1, RW), c2),
                  pl.BlockSpec((RW, RW), c2)],
        out_specs=[pl.BlockSpec((TM, RW), row)] * 11,
        out_shape=[out] * 11,
        compiler_params=_cp(("arbitrary",)),
        name="rwkv_prep",
    )(p_rw, p_rw, p_rw, mup, mun, wdec, wicl, gup, w0, a0, k_k, k_a, r_k, bd512)


def _bd(x, bdmask):
    return jnp.where(bdmask, jnp.concatenate([x, x, x, x], axis=0), 0.0)


def _wkv_pre(sl, r_ref, v_ref, kk_ref, lw_ref, k_ref, b_ref, tri_ref, rev, cst):
    strict, incl, eye_w, bdmask = cst
    r = r_ref[sl, :]
    v = v_ref[sl, :]
    kk = kk_ref[sl, :]
    lw = lw_ref[sl, :]
    k = k_ref[sl, :]
    b = b_ref[sl, :]
    tri = tri_ref[...]
    l3 = _split3(lw)
    cs = _dot(tri, l3[0]) + _dot(tri, l3[1]) + _dot(tri, l3[2])
    yield
    tot = cs[0:1, :] if rev else cs[CHUNK - 1:CHUNK, :]
    e_neg = jnp.exp(-cs)
    e_end = jnp.exp(tot - cs)
    rq = r * jnp.exp(cs)
    kq = kk * jnp.exp(cs - lw)

    lhs1 = jnp.concatenate([kq, rq], axis=0).astype(BF16)
    rhs1 = jnp.concatenate([_bd(b * e_neg, bdmask), _bd(k * e_neg, bdmask)], axis=0).astype(BF16)
    a_all = _dot_nt(lhs1, rhs1)
    yield
    n_ub = jnp.where(strict, a_all[0:CHUNK, 0:HG], 0.0)
    a_uk = jnp.where(strict, a_all[0:CHUNK, HG:2 * HG], 0.0)
    a_rb = jnp.where(incl, a_all[CHUNK:2 * CHUNK, 0:HG], 0.0).astype(BF16)
    a_rk = jnp.where(incl, a_all[CHUNK:2 * CHUNK, HG:2 * HG], 0.0)

    n_sq = int(math.log2(CHUNK))
    m_j = -n_ub
    t_inv = jnp.where(eye_w, 1.0, 0.0) + m_j
    m_j = _dot(m_j.astype(BF16), _bd(m_j, bdmask).astype(BF16))
    yield
    for j in range(1, n_sq):
        mb = _bd(m_j, bdmask).astype(BF16)
        if j < n_sq - 1:
            both = _dot(jnp.concatenate([t_inv, m_j], axis=0).astype(BF16), mb)
            t_inv = t_inv + both[0:CHUNK]
            m_j = both[CHUNK:2 * CHUNK]
        else:
            t_inv = t_inv + _dot(t_inv.astype(BF16), mb)
        yield

    av = _dot(jnp.concatenate([a_uk, a_rk], axis=0).astype(BF16), _bd(v, bdmask).astype(BF16))
    yield
    tb = t_inv.astype(BF16)
    u0 = -_dot(tb, _bd(av[0:CHUNK], bdmask).astype(BF16))
    pm = _dot(tb, _bd(kq, bdmask).astype(BF16))
    yield
    lhs_t = jnp.concatenate([b * e_end, k * e_end], axis=0).T.astype(BF16)
    wcol = jnp.exp(jnp.broadcast_to(tot, (LANES, HG)).T)
    return dict(u0=u0, pmrq=jnp.concatenate([pm, rq], axis=0).astype(BF16), a_rb=a_rb, yv=av[CHUNK:2 * CHUNK],
                v=v.astype(BF16), lhs_t=lhs_t, wcol=jnp.concatenate([wcol, wcol], axis=1))


def _wkv_seq(sl, pre, st_ref, o_ref, bdmask):
    st = st_ref[...]
    g = _dot(pre["pmrq"], st.astype(BF16))
    yield
    u = pre["u0"] - g[0:CHUNK]
    upd = _dot(pre["lhs_t"], jnp.concatenate([u.astype(BF16), pre["v"]], axis=0))
    yield
    st_ref[...] = st * pre["wcol"] + jnp.where(bdmask, upd, 0.0)
    o_ref[sl, :] = g[CHUNK:2 * CHUNK] + pre["yv"] + _dot(pre["a_rb"], _bd(u, bdmask).astype(BF16))


def _lockstep(gens):
    results = [None] * len(gens)
    live = list(range(len(gens)))
    while live:
        for i in list(live):
            try:
                next(gens[i])
            except StopIteration as stop:
                results[i] = stop.value
                live.remove(i)
    return results


def _wkv_kernel(rf, vf, kkf, lwf, kf, bf, rb, vb, kkb, lwb, kb, bb, trif, trib, of, ob, stf, stb):
    @pl.when(pl.program_id(2) == 0)
    def _():
        stf[...] = jnp.zeros_like(stf)
        stb[...] = jnp.zeros_like(stb)

    row = lax.broadcasted_iota(jnp.int32, (CHUNK, HG), 0)
    s_idx = lax.broadcasted_iota(jnp.int32, (CHUNK, HG), 1) & (HD - 1)
    r2 = lax.broadcasted_iota(jnp.int32, (HG, HG), 0) >> 6
    c2 = lax.broadcasted_iota(jnp.int32, (HG, HG), 1) >> 6
    bdmask = r2 == c2
    eye_w = s_idx == row
    cst_f = (s_idx < row, s_idx <= row, eye_w, bdmask)
    cst_b = (s_idx > row, s_idx >= row, eye_w, bdmask)
    nc = WKV_BLK // CHUNK
    sls = [pl.ds(c * CHUNK, CHUNK) for c in range(nc)]
    pre = _lockstep([_wkv_pre(sl, rf, vf, kkf, lwf, kf, bf, trif, False, cst_f) for sl in sls]
                    + [_wkv_pre(sl, rb, vb, kkb, lwb, kb, bb, trib, True, cst_b) for sl in sls])
    for c in range(nc):
        _lockstep([_wkv_seq(sls[c], pre[c], stf, of, bdmask),
                   _wkv_seq(sls[nc - 1 - c], pre[nc + nc - 1 - c], stb, ob, bdmask)])


def _wkv(r, v, kk, lwf, lwb, kf, kb, bf, bb, trif, trib, bsz, n_tok):
    nb = n_tok // WKV_BLK
    sh3 = lambda a: a.reshape(bsz, n_tok, RW)
    fwd = pl.BlockSpec((None, WKV_BLK, HG), lambda b, g, i: (b, i, g))
    bwd = pl.BlockSpec((None, WKV_BLK, HG), lambda b, g, i: (b, nb - 1 - i, g))
    tri = pl.BlockSpec((CHUNK, CHUNK), lambda b, g, i: (0, 0))
    out = jax.ShapeDtypeStruct((bsz, n_tok, RW), F32)
    of, ob = pl.pallas_call(
        _wkv_kernel,
        grid=(bsz, RW // HG, nb),
        in_specs=[fwd] * 6 + [bwd] * 6 + [tri, tri],
        out_specs=[fwd, bwd],
        out_shape=[out, out],
        scratch_shapes=[pltpu.VMEM((HG, HG), F32), pltpu.VMEM((HG, HG), F32)],
        compiler_params=_cp(("arbitrary", "arbitrary", "arbitrary")),
        name="wkv",
    )(sh3(r), sh3(v), sh3(kk), sh3(lwf), sh3(kf), sh3(bf),
      sh3(r), sh3(v), sh3(kk), sh3(lwb), sh3(kb), sh3(bb), trif, trib)
    return of.reshape(-1, RW), ob.reshape(-1, RW)


def _attn_kernel(q_ref, k_ref, v_ref, o_ref, m_scr, acc_scr, *, n_kb, tk):
    q = q_ref[...]
    m_scr[...] = jnp.full_like(m_scr, NEG_BIG)
    acc_scr[...] = jnp.zeros_like(acc_scr)

    def body(kb, carry):
        sl = pl.ds(pl.multiple_of(kb * tk, tk), tk)
        s = _dot_nt(q, k_ref[sl, :])
        m_prev = m_scr[...]
        m_new = jnp.maximum(m_prev, jnp.max(s, axis=1, keepdims=True))
        alpha = jnp.exp2(m_prev - m_new)
        p = jnp.exp2(s - jnp.tile(m_new, (1, tk // LANES)))
        acc_scr[...] = alpha * acc_scr[...] + _dot(p.astype(PV_DT), v_ref[sl, :])
        m_scr[...] = m_new
        return carry

    lax.fori_loop(0, n_kb, body, 0, unroll=2 if n_kb % 2 == 0 else 1)
    acc = acc_scr[...]
    o = acc / acc[:, HD:HD + 1]
    low = lax.broadcasted_iota(jnp.int32, (TM, LANES), 1) < HD
    for pr in range(2):
        oa = o[(2 * pr) * TM:(2 * pr + 1) * TM, :]
        ob = o[(2 * pr + 1) * TM:(2 * pr + 2) * TM, :]
        o_ref[:, pr * LANES:(pr + 1) * LANES] = jnp.where(low, oa, pltpu.roll(ob, HD, axis=1)).astype(BF16)


def _attention(q_st, k, v, bsz, n_tok):
    nq = n_tok // TM
    tk = min(ATT_TK, n_tok)
    q5 = q_st.reshape(bsz, nq, 2, 4 * TM, LANES)
    k4 = k.reshape(2, bsz, n_tok, LANES)
    v4 = v.reshape(2, bsz, n_tok, LANES)
    kv = pl.BlockSpec((None, None, n_tok, LANES), lambda b, g, i: (g, b, 0, 0))
    o = pl.pallas_call(
        functools.partial(_attn_kernel, n_kb=n_tok // tk, tk=tk),
        grid=(bsz, 2, nq),
        in_specs=[pl.BlockSpec((None, None, None, 4 * TM, LANES), lambda b, g, i: (b, i, g, 0, 0)), kv, kv],
        out_specs=pl.BlockSpec((None, TM, 2 * LANES), lambda b, g, i: (b, i, g)),
        out_shape=jax.ShapeDtypeStruct((bsz, n_tok, RW), BF16),
        scratch_shapes=[pltpu.VMEM((4 * TM, LANES), F32)] * 2,
        compiler_params=_cp(("arbitrary", "arbitrary", "arbitrary"), 56),
        name="attention",
    )(q5, k4, v4)
    return o.reshape(-1, RW)


def _merge_kernel(of_ref, ob_ref, bv_ref, go_ref, oat_ref, gate_ref, x_ref, gt1_ref, sc2_ref, sh2_ref,
                  lnw_ref, lnb_ref, g2_ref, wbr_ref, wba_ref, wout_ref, wrh_ref, wrl_ref, br_ref,
                  bd_ref, tri_ref, x1_ref, h2_ref, route_ref, cnt_ref, carry):
    @pl.when(pl.program_id(0) == 0)
    def _():
        carry[...] = jnp.zeros_like(carry)

    bd = bd_ref[...]
    o = of_ref[...] + ob_ref[...]
    oh, om, _ = _split3(o)
    mu = (_dot(oh, bd) + _dot(om, bd)) * (1.0 / HD)
    d = o - mu
    dh, dm, _ = _split3(d * d)
    var = (_dot(dh, bd) + _dot(dm, bd)) * (1.0 / HD)
    on = d * lax.rsqrt(var + GN_EPS) * lnw_ref[...] + lnb_ref[...]
    o_rw = (on + bv_ref[...]) * go_ref[...]
    br = _dot(o_rw.astype(BF16), wbr_ref[...])
    ba = _dot(oat_ref[...], wba_ref[...])
    gates = gate_ref[...]
    merged = gates[:, 0:D].astype(F32) * br + gates[:, D:2 * D].astype(F32) * ba
    x1 = x_ref[...] + gt1_ref[...] * _dot(merged.astype(BF16), wout_ref[...])
    x1_ref[...] = x1
    ms = jnp.mean(x1 * x1, axis=-1, keepdims=True)
    h2 = x1 * lax.rsqrt(ms + EPS) * g2_ref[...]
    h2 = h2 * (1.0 + sc2_ref[...]) + sh2_ref[...]
    _store_rows(h2_ref, h2, TM)

    hh, hm, hl = _split3(h2)
    wh = wrh_ref[...]
    wl = wrl_ref[...]
    logits = _dot(hh, wh) + (_dot(hh, wl) + _dot(hm, wh)) + (_dot(hm, wl) + _dot(hl, wh)) + br_ref[...]

    lane = lax.broadcasted_iota(jnp.int32, (TM, LANES), 1).astype(F32)
    cur = logits
    vals, idxs = [], []
    for _ in range(TOP_K):
        m = jnp.max(cur, axis=1, keepdims=True)
        ix = jnp.min(jnp.where(cur == m, lane, float(LANES)), axis=1, keepdims=True)
        vals.append(m)
        idxs.append(ix)
        cur = jnp.where(lane == ix, -jnp.inf, cur)
    es = [jnp.exp(vv - vals[0]) for vv in vals]
    den = es[0] + es[1] + es[2] + es[3]
    onehot = jnp.zeros((TM, LANES), F32)
    for ix in idxs:
        onehot = onehot + jnp.where(lane == ix, 1.0, 0.0)
    cnt = _dot(tri_ref[...], onehot.astype(BF16)) + carry[...]
    route = jnp.zeros((TM, LANES), F32)
    for j in range(TOP_K):
        rank = jnp.sum(jnp.where(lane == idxs[j], cnt, 0.0), axis=1, keepdims=True)
        route = jnp.where(lane == float(j), idxs[j], route)
        route = jnp.where(lane == float(TOP_K + j), rank, route)
        route = jnp.where(lane == float(2 * TOP_K + j), es[j] / den, route)
    route_ref[...] = route
    carry[...] = carry[...] + jnp.sum(onehot, axis=0, keepdims=True)
    cnt_ref[...] = jnp.broadcast_to(carry[...], (8, LANES))


def _merge(of, ob, bv, go, oat, gates, x2, gt1, sc2, sh2, lnw, lnb, g2, wbr, wba, wout, wrh, wrl, brt,
           bd512, tri_tm, n_tok):
    n = x2.shape[0]
    tpb = n_tok // TM
    row = lambda i: (i, 0)
    per_b = lambda i: (i // tpb, 0, 0)
    c2 = lambda i: (0, 0)
    rw = pl.BlockSpec((TM, RW), row)
    full = lambda a: pl.BlockSpec(a.shape, c2)
    mod = pl.BlockSpec((None, 1, D), per_b)
    return pl.pallas_call(
        _merge_kernel,
        grid=(n // TM,),
        in_specs=[rw, rw, rw, rw, pl.BlockSpec((TM, RW), row), pl.BlockSpec((TM, 2 * D), row),
                  pl.BlockSpec((TM, D), row), mod, mod, mod,
                  full(lnw), full(lnb), full(g2), full(wbr), full(wba), full(wout), full(wrh), full(wrl),
                  full(brt), full(bd512), full(tri_tm)],
        out_specs=[pl.BlockSpec((TM, D), row), pl.BlockSpec((TM * ROW_SUB, LANES), row),
                   pl.BlockSpec((TM, LANES), row),
                   pl.BlockSpec((8, LANES), c2)],
        out_shape=[jax.ShapeDtypeStruct((n, D), F32), jax.ShapeDtypeStruct((n * ROW_SUB, LANES), F32),
                   jax.ShapeDtypeStruct((n, LANES), F32), jax.ShapeDtypeStruct((8, LANES), F32)],
        scratch_shapes=[pltpu.VMEM((1, LANES), F32)],
        compiler_params=_cp(("arbitrary",)),
        name="merge",
    )(of, ob, bv, go, oat, gates, x2, gt1, sc2, sh2, lnw, lnb, g2, wbr, wba, wout, wrh, wrl, brt,
      bd512, tri_tm)


def _slots_kernel(route_ref, pstart_ref, o_ref):
    route = route_ref[...]
    ps = pstart_ref[...]
    lane = lax.broadcasted_iota(jnp.int32, route.shape, 1).astype(F32)
    out = jnp.zeros(route.shape, F32)
    for j in range(TOP_K):
        off = jnp.sum(jnp.where(lane == route[:, j:j + 1], ps, 0.0), axis=1, keepdims=True)
        out = jnp.where(lane == float(j), off + route[:, TOP_K + j:TOP_K + j + 1], out)
    o_ref[...] = out.astype(jnp.int32)


def _slots(route, pstart_row):
    n = route.shape[0]
    return pl.pallas_call(
        _slots_kernel,
        grid=(n // SLOT_TM,),
        in_specs=[pl.BlockSpec((SLOT_TM, LANES), lambda i: (i, 0)), pl.BlockSpec((1, LANES), lambda i: (0, 0))],
        out_specs=pl.BlockSpec((SLOT_TM, LANES), lambda i: (i, 0)),
        out_shape=jax.ShapeDtypeStruct((n, LANES), jnp.int32),
        compiler_params=_cp(("arbitrary",)),
        name="moe_slots",
    )(route, pstart_row)


def _dispatch_kernel(slot_ref, pend_ref, npad_ref, h2_ref, xs_hbm, zbuf, sem):
    @pl.when(pl.program_id(0) == 0)
    def _():
        zbuf[...] = jnp.zeros_like(zbuf)

        def fill_tail(blk, carry):
            first = pl.multiple_of(blk * (MOE_BM * ROW_SUB), MOE_BM * ROW_SUB)
            fill = pltpu.make_async_copy(zbuf, xs_hbm.at[pl.ds(first, MOE_BM * ROW_SUB)], sem)
            fill.start()
            fill.wait()
            return carry

        lax.fori_loop(pend_ref[N_EXPERTS - 1] // MOE_BM, xs_hbm.shape[0] // (MOE_BM * ROW_SUB), fill_tail, 0)
        for e in range(N_EXPERTS):
            @pl.when(npad_ref[e] > 0)
            def _():
                last = pl.multiple_of((pend_ref[e] - MOE_BM) * ROW_SUB, MOE_BM * ROW_SUB)
                fill = pltpu.make_async_copy(zbuf, xs_hbm.at[pl.ds(last, MOE_BM * ROW_SUB)], sem)
                fill.start()
                fill.wait()

    def body(t, carry):
        for j in range(TOP_K):
            pltpu.make_async_copy(h2_ref.at[_tile(t)], xs_hbm.at[_tile(slot_ref[t * TOP_K + j])],
                                  sem).start(priority=j % 2)
        return carry

    lax.fori_loop(0, TM, body, 0, unroll=DMA_UNROLL)
    for _ in range(TOP_K):
        pltpu.make_async_copy(h2_ref, xs_hbm.at[pl.ds(0, TM * ROW_SUB)], sem).wait()


def _dispatch(slot_flat, pad_end, padded, h2, cap):
    n = h2.shape[0] // ROW_SUB
    return pl.pallas_call(
        _dispatch_kernel,
        grid=(n // TM,),
        in_specs=[pl.BlockSpec((TM * TOP_K,), lambda i: (i,), memory_space=pltpu.SMEM),
                  pl.BlockSpec(memory_space=pltpu.SMEM),
                  pl.BlockSpec(memory_space=pltpu.SMEM),
                  pl.BlockSpec((TM * ROW_SUB, LANES), lambda i: (i, 0))],
        out_specs=pl.BlockSpec(memory_space=pl.ANY),
        out_shape=jax.ShapeDtypeStruct((cap * ROW_SUB, LANES), F32),
        scratch_shapes=[pltpu.VMEM((MOE_BM * ROW_SUB, LANES), F32), pltpu.SemaphoreType.DMA(())],
        compiler_params=_cp(("arbitrary",)),
        name="moe_dispatch",
    )(slot_flat, pad_end, padded, h2)


def _expert_kernel(bexp_ref, nused_ref, xs_ref, wgu_ref, bgu_ref, wd_ref, bdn_ref, ys_ref):
    i = pl.program_id(0)

    @pl.when(i < nused_ref[0])
    def _():
        gu = _dot(_load_rows(xs_ref, MOE_BM).astype(BF16), wgu_ref[...]) + bgu_ref[...]
        g_lin = jnp.minimum(gu[:, 0:D_FF], SWIGLU_LIMIT)
        u_lin = jnp.clip(gu[:, D_FF:2 * D_FF], -SWIGLU_LIMIT, SWIGLU_LIMIT)
        act = (u_lin + 1.0) * (g_lin * jax.nn.sigmoid(SWIGLU_ALPHA * g_lin))
        _store_rows(ys_ref, _dot(act.astype(BF16), wd_ref[...]) + bdn_ref[...], MOE_BM)

    @pl.when(i >= nused_ref[0])
    def _():
        ys_ref[...] = jnp.zeros_like(ys_ref)


def _experts(blk_exp, nused, xs, wgu_b, bgu, wd_b, bdn):
    cap = xs.shape[0] // ROW_SUB
    grid_spec = pltpu.PrefetchScalarGridSpec(
        num_scalar_prefetch=2,
        grid=(cap // MOE_BM,),
        in_specs=[pl.BlockSpec((MOE_BM * ROW_SUB, LANES), lambda i, be, nu: (jnp.minimum(i, nu[0] - 1), 0)),
                  pl.BlockSpec((None, D, 2 * D_FF), lambda i, be, nu: (be[i], 0, 0)),
                  pl.BlockSpec((None, 1, 2 * D_FF), lambda i, be, nu: (be[i], 0, 0)),
                  pl.BlockSpec((None, D_FF, D), lambda i, be, nu: (be[i], 0, 0)),
                  pl.BlockSpec((None, 1, D), lambda i, be, nu: (be[i], 0, 0))],
        out_specs=pl.BlockSpec((MOE_BM * ROW_SUB, LANES), lambda i, be, nu: (i, 0)),
    )
    return pl.pallas_call(
        _expert_kernel,
        grid_spec=grid_spec,
        out_shape=jax.ShapeDtypeStruct(xs.shape, F32),
        compiler_params=_cp(("arbitrary",), 56),
        name="moe_experts",
    )(blk_exp, nused, xs, wgu_b, bgu, wd_b, bdn)


def _combine_kernel(slot_ref, ys_hbm, route_ref, x1_ref, gt2_ref, gf_ref, o_ref, buf, sem):
    def body(t, carry):
        for j in range(TOP_K):
            pltpu.make_async_copy(ys_hbm.at[_tile(slot_ref[t * TOP_K + j])], buf.at[j, _tile(t)],
                                  sem).start(priority=j % 2)
        return carry

    lax.fori_loop(0, TM, body, 0, unroll=DMA_UNROLL)
    for j in range(TOP_K):
        pltpu.make_async_copy(ys_hbm.at[pl.ds(0, TM * ROW_SUB)], buf.at[j], sem).wait()
    route = route_ref[...]
    y = jnp.zeros((TM, D), F32)
    for j in range(TOP_K):
        y = y + route[:, 2 * TOP_K + j:2 * TOP_K + j + 1] * _load_rows(buf, TM, lead=(j,))
    x = x1_ref[...] + gt2_ref[...] * y
    ms = jnp.mean(x * x, axis=-1, keepdims=True)
    o_ref[...] = x * lax.rsqrt(ms + EPS) * gf_ref[...]


def _combine(slot_flat, ys, route, x1, gt2, gf, n_tok):
    n = x1.shape[0]
    tpb = n_tok // TM
    row = lambda i: (i, 0)
    return pl.pallas_call(
        _combine_kernel,
        grid=(n // TM,),
        in_specs=[pl.BlockSpec((TM * TOP_K,), lambda i: (i,), memory_space=pltpu.SMEM),
                  pl.BlockSpec(memory_space=pl.ANY),
                  pl.BlockSpec((TM, LANES), row),
                  pl.BlockSpec((TM, D), row),
                  pl.BlockSpec((None, 1, D), lambda i: (i // tpb, 0, 0)),
                  pl.BlockSpec((1, D), lambda i: (0, 0))],
        out_specs=pl.BlockSpec((TM, D), row),
        out_shape=jax.ShapeDtypeStruct((n, D), F32),
        scratch_shapes=[pltpu.VMEM((TOP_K, TM * ROW_SUB, LANES), F32), pltpu.SemaphoreType.DMA(())],
        compiler_params=_cp(("arbitrary",)),
        name="moe_combine",
    )(slot_flat, ys, route, x1, gt2, gf)


def _rope_tables(n_tok):
    t = jnp.arange(n_tok)
    row = (t // GRID_W).astype(F32)
    col = (t % GRID_W).astype(F32)
    nf = HD // 4
    freqs = ROPE_THETA ** (-jnp.arange(nf, dtype=F32) / nf)
    ang = jnp.concatenate([row[:, None] * freqs, col[:, None] * freqs], axis=-1)
    cos = jnp.repeat(jnp.cos(ang), 2, axis=-1)
    sin = jnp.repeat(jnp.sin(ang), 2, axis=-1)
    sign = jnp.tile(jnp.array([-1.0, 1.0], F32), HD // 2)
    return jnp.tile(cos, (1, 2)), jnp.tile(sin * sign, (1, 2))


def _block_diag2(a, b):
    z = jnp.zeros_like(a)
    return jnp.concatenate([jnp.concatenate([a, z], axis=1), jnp.concatenate([z, b], axis=1)], axis=0)


def _prepare_weights(w):
    f = {}
    f["w_in"] = w["w_in"][0].astype(BF16)
    f["g1"] = w["norm1_g"][0].reshape(1, D)
    f["g2"] = w["norm2_g"][0].reshape(1, D)
    f["qg"] = jnp.tile(w["q_norm_g"][0], 2).reshape(1, LANES)
    f["kg"] = jnp.tile(w["k_norm_g"][0], 2).reshape(1, LANES)
    ones = jnp.ones((HD, HD), F32)
    f["bd128"] = jnp.kron(jnp.eye(2, dtype=F32), ones).astype(BF16)
    f["bd512"] = jnp.kron(jnp.eye(NH, dtype=F32), ones).astype(BF16)
    f["mup"] = w["mu_prev"][0].reshape(1, RW_COLS)
    f["mun"] = w["mu_next"][0].reshape(1, RW_COLS)
    f["wdec"] = _block_diag2(w["wb_f"][0], w["wb_b"][0]).astype(BF16)
    f["wicl"] = _block_diag2(w["ab_f"][0], w["ab_b"][0]).astype(BF16)
    f["gup"] = w["g_up"][0].astype(BF16)
    f["w0"] = jnp.concatenate([w["w0_f"][0], w["w0_b"][0]]).reshape(1, 2 * RW)
    f["a0"] = jnp.concatenate([w["a0_f"][0], w["a0_b"][0]]).reshape(1, 2 * RW)
    f["k_k"] = w["k_k"][0].reshape(1, RW)
    f["k_a"] = w["k_a"][0].reshape(1, RW)
    f["r_k"] = w["r_k"][0].reshape(1, RW)
    ti = jnp.arange(CHUNK)
    f["trif"] = (ti[None, :] <= ti[:, None]).astype(BF16)
    f["trib"] = (ti[None, :] >= ti[:, None]).astype(BF16)
    tm = jnp.arange(TM)
    f["tri_tm"] = (tm[None, :] < tm[:, None]).astype(BF16)
    f["lnw"] = w["lnx_w"][0].reshape(1, RW)
    f["lnb"] = w["lnx_b"][0].reshape(1, RW)
    f["wbr"] = w["w_br_rwkv"][0].astype(BF16)
    f["wba"] = w["w_br_attn"][0].astype(BF16)
    f["wout"] = w["w_out"][0].astype(BF16)
    wr = jnp.pad(w["w_router"][0], ((0, 0), (0, LANES - N_EXPERTS)))
    wrh = wr.astype(BF16)
    f["wrh"] = wrh
    f["wrl"] = (wr - wrh.astype(F32)).astype(BF16)
    f["br"] = jnp.pad(w["b_router"][0], (0, LANES - N_EXPERTS), constant_values=NEG_BIG).reshape(1, LANES)
    f["wgu"] = w["w_gu"][0].astype(BF16)
    f["bgu"] = w["b_gu"][0].reshape(N_EXPERTS, 1, 2 * D_FF)
    f["wd"] = w["w_down"][0].astype(BF16)
    f["bdn"] = w["b_down"][0].reshape(N_EXPERTS, 1, D)
    f["gf"] = w["normf_g"].reshape(1, D)
    return f


def _run(x, c, w, f):
    bsz, n_tok, _ = x.shape
    n = bsz * n_tok
    x2 = x.reshape(n, D)
    c8 = jnp.pad(c, ((0, 8 - bsz), (0, 0)))
    mod = _ada(c8, w["w_ada"][0], w["b_ada"][0])[:bsz]
    sh1, sc1, gt1, sh2, sc2, gt2 = [m.reshape(bsz, 1, D) for m in jnp.split(mod, 6, axis=-1)]

    cos_t, sin_t = _rope_tables(n_tok)
    p_rw, q_st, k_att, v_att, gates = _inproj(x2, sc1, sh1, f["g1"], f["w_in"], cos_t, sin_t,
                                              f["qg"], f["kg"], f["bd128"], n_tok)
    r, v, kk, lwf, lwb, kf, kb, bf, bb, go, bv = _prep(
        p_rw, f["mup"], f["mun"], f["wdec"], f["wicl"], f["gup"], f["w0"], f["a0"],
        f["k_k"], f["k_a"], f["r_k"], f["bd512"], n_tok)
    of, ob = _wkv(r, v, kk, lwf, lwb, kf, kb, bf, bb, f["trif"], f["trib"], bsz, n_tok)
    o_at = _attention(q_st, k_att, v_att, bsz, n_tok)
    x1, h2, route, cnt = _merge(of, ob, bv, go, o_at, gates, x2, gt1, sc2, sh2, f["lnw"], f["lnb"], f["g2"],
                                f["wbr"], f["wba"], f["wout"], f["wrh"], f["wrl"], f["br"],
                                f["bd512"], f["tri_tm"], n_tok)

    counts = cnt[0, :N_EXPERTS].astype(jnp.int32)
    padded = ((counts + MOE_BM - 1) // MOE_BM * MOE_BM).astype(jnp.int32)
    pad_end = jnp.cumsum(padded).astype(jnp.int32)
    pstart_row = jnp.pad((pad_end - padded).astype(F32), (0, LANES - N_EXPERTS)).reshape(1, LANES)
    nk = n * TOP_K
    nblk = -(-nk // MOE_BM) + N_EXPERTS
    blk_first = (jnp.arange(nblk) * MOE_BM)[:, None]
    blk_exp = jnp.minimum(jnp.sum(pad_end[None, :] <= blk_first, axis=1), N_EXPERTS - 1).astype(jnp.int32)
    nused = (pad_end[-1:] // MOE_BM).astype(jnp.int32)

    slot_flat = _slots(route, pstart_row)[:, 0:TOP_K].reshape(nk)
    xs = _dispatch(slot_flat, pad_end, padded, h2, nblk * MOE_BM)
    ys = _experts(blk_exp, nused, xs, f["wgu"], f["bgu"], f["wd"], f["bdn"])
    y = _combine(slot_flat, ys, route, x1, gt2, f["gf"], n_tok)
    return y.reshape(bsz, n_tok, D)


def kernel(x_prompt, x_sample, c_prompt, c_sample, norm1_g, norm2_g, w_ada, b_ada, w_in, mu_prev, mu_next, w0_f, w0_b, wb_f, wb_b, a0_f, a0_b, ab_f, ab_b, k_k, k_a, r_k, g_up, lnx_w, lnx_b, q_norm_g, k_norm_g, w_br_rwkv, w_br_attn, w_out, w_router, b_router, w_gu, b_gu, w_down, b_down, normf_g):
    w = dict(norm1_g=norm1_g, norm2_g=norm2_g, w_ada=w_ada, b_ada=b_ada, w_in=w_in, mu_prev=mu_prev,
             mu_next=mu_next, w0_f=w0_f, w0_b=w0_b, wb_f=wb_f, wb_b=wb_b, a0_f=a0_f, a0_b=a0_b, ab_f=ab_f,
             ab_b=ab_b, k_k=k_k, k_a=k_a, r_k=r_k, g_up=g_up, lnx_w=lnx_w, lnx_b=lnx_b, q_norm_g=q_norm_g,
             k_norm_g=k_norm_g, w_br_rwkv=w_br_rwkv, w_br_attn=w_br_attn, w_out=w_out, w_router=w_router,
             b_router=b_router, w_gu=w_gu, b_gu=b_gu, w_down=w_down, b_down=b_down, normf_g=normf_g)
    f = _prepare_weights(w)
    return (_run(x_prompt, c_prompt, w, f), _run(x_sample, c_sample, w, f))
```

```python
import functools
import math

import jax
import jax.numpy as jnp
from jax import lax
from jax.experimental import pallas as pl
from jax.experimental.pallas import tpu as pltpu

F32 = jnp.float32
BF16 = jnp.bfloat16
ATT_DT = jnp.float8_e4m3fn
PV_DT = jnp.bfloat16

D = 1024
GRID_W = 64
NH = 8
HD = 64
RW = NH * HD
RW_COLS = 1920
Q_OFF, K_OFF, V_OFF, G_OFF, IN_COLS = 1920, 2432, 2560, 2688, 4736
DECAY_SCALE = math.exp(-0.5)
GN_EPS = 64e-5
EPS = 1e-6
ROPE_THETA = 10000.0
N_EXPERTS = 32
TOP_K = 4
D_FF = 1024
SWIGLU_LIMIT = 7.0
SWIGLU_ALPHA = 1.702

TM = 256
MERGE_TM = 512
CHUNK = 64
WKV_BLK = 256
HG = 256
ATT_TK = 2048
MOE_BM = 512
SLOT_TM = 1024
DMA_UNROLL = 4
ROW_SUB = 8
LANES = 128
NEG_BIG = -1e30


def _cp(sem, vmem_mb=48):
    return pltpu.CompilerParams(dimension_semantics=sem, vmem_limit_bytes=vmem_mb << 20)


def _split3(x):
    h = x.astype(BF16)
    r1 = x - h.astype(F32)
    m = r1.astype(BF16)
    lo = (r1 - m.astype(F32)).astype(BF16)
    return h, m, lo


def _dot(a, b):
    return jnp.dot(a, b, preferred_element_type=F32)


def _dot_nt(a, b):
    return lax.dot_general(a, b, (((1,), (1,)), ((), ())), preferred_element_type=F32)


def _group_sum(x, bd):
    return jnp.concatenate([_dot(x[:, 0:HG], bd), _dot(x[:, HG:2 * HG], bd)], axis=1)


def _tile(r):
    return pl.ds(pl.multiple_of(r * ROW_SUB, ROW_SUB), ROW_SUB)


def _load_rows(ref, n_rows, lead=()):
    return jnp.concatenate([ref[lead + (pl.ds(sub, n_rows, stride=ROW_SUB), slice(None))]
                            for sub in range(ROW_SUB)], axis=1)


def _store_rows(ref, val, n_rows):
    for sub in range(ROW_SUB):
        ref[pl.ds(sub, n_rows, stride=ROW_SUB), :] = val[:, sub * LANES:(sub + 1) * LANES]


def _ada_kernel(c_ref, w_ref, b_ref, o_ref):
    c = c_ref[...]
    s = c * jax.nn.sigmoid(c)
    sh, sm, sl = _split3(s)
    wh, wm, wl = _split3(w_ref[...])
    acc = _dot(sh, wh) + (_dot(sh, wm) + _dot(sm, wh)) + (_dot(sm, wm) + _dot(sh, wl) + _dot(sl, wh))
    o_ref[...] = acc + b_ref[...]


def _ada(c8, w_ada, b_ada):
    n_mod = w_ada.shape[1] // D
    return pl.pallas_call(
        _ada_kernel,
        grid=(n_mod,),
        in_specs=[pl.BlockSpec((8, D), lambda j: (0, 0)),
                  pl.BlockSpec((D, D), lambda j: (0, j)),
                  pl.BlockSpec((1, D), lambda j: (0, j))],
        out_specs=pl.BlockSpec((8, D), lambda j: (0, j)),
        out_shape=jax.ShapeDtypeStruct((8, n_mod * D), F32),
        compiler_params=_cp(("arbitrary",)),
        name="ada",
    )(c8, w_ada, b_ada.reshape(1, -1))


def _rope(x, c, s, even):
    swap = jnp.where(even, pltpu.roll(x, LANES - 1, axis=1), pltpu.roll(x, 1, axis=1))
    return x * c + swap * s


def _inproj_kernel(x_ref, sc_ref, sh_ref, g_ref, w_ref, cos_ref, sin_ref, qg_ref, kg_ref, bd_ref,
                   prw_ref, q_ref, k_ref, v_ref, gate_ref):
    x = x_ref[...]
    ms = jnp.mean(x * x, axis=-1, keepdims=True)
    h = x * lax.rsqrt(ms + EPS) * g_ref[...]
    h = h * (1.0 + sc_ref[...]) + sh_ref[...]
    hb = h.astype(BF16)
    prw_ref[...] = _dot(hb, w_ref[:, 0:RW_COLS])
    gate_ref[...] = jax.nn.sigmoid(_dot(hb, w_ref[:, G_OFF:IN_COLS])).astype(BF16)

    cos = cos_ref[...]
    sin = sin_ref[...]
    bd = bd_ref[...]
    lane = lax.broadcasted_iota(jnp.int32, (TM, LANES), 1)
    even = (lane & 1) == 0
    low = lane < HD

    def norm_rope(slab, gain):
        ssq = _dot((slab * slab).astype(BF16), bd)
        return _rope(slab * lax.rsqrt(ssq * (1.0 / HD) + EPS) * gain, cos, sin, even)

    kr = norm_rope(_dot(hb, w_ref[:, K_OFF:V_OFF]), kg_ref[...])
    k_ref[0] = jnp.where(low, kr, 0.0).astype(ATT_DT)
    k_ref[1] = jnp.where(low, pltpu.roll(kr, HD, axis=1), 0.0).astype(ATT_DT)
    vf = _dot(hb, w_ref[:, V_OFF:G_OFF])
    one_col = jnp.where(lane == HD, 1.0, 0.0)
    v_ref[0] = jnp.where(low, vf, one_col).astype(PV_DT)
    v_ref[1] = jnp.where(low, pltpu.roll(vf, HD, axis=1), one_col).astype(PV_DT)

    qf = _dot(hb, w_ref[:, Q_OFF:K_OFF])
    scale = HD ** -0.5 * math.log2(math.e)
    for j in range(NH // 2):
        slab = norm_rope(qf[:, j * LANES:(j + 1) * LANES], qg_ref[...]) * scale
        g, ha = (2 * j) // 4, (2 * j) % 4
        q_ref[g, ha * TM:(ha + 1) * TM, :] = jnp.where(low, slab, 0.0).astype(ATT_DT)
        q_ref[g, (ha + 1) * TM:(ha + 2) * TM, :] = jnp.where(low, pltpu.roll(slab, HD, axis=1), 0.0).astype(ATT_DT)


def _inproj(x2, sc1, sh1, g1, w_in_b, cos_t, sin_t, qg, kg, bd128, n_tok):
    n = x2.shape[0]
    tpb = n_tok // TM
    row = lambda i: (i, 0)
    per_b = lambda i: (i // tpb, 0, 0)
    const2 = lambda i: (0, 0)
    tab = lambda i: (i % tpb, 0)
    return pl.pallas_call(
        _inproj_kernel,
        grid=(n // TM,),
        in_specs=[pl.BlockSpec((TM, D), row),
                  pl.BlockSpec((None, 1, D), per_b),
                  pl.BlockSpec((None, 1, D), per_b),
                  pl.BlockSpec((1, D), const2),
                  pl.BlockSpec((D, IN_COLS), const2),
                  pl.BlockSpec((TM, LANES), tab),
                  pl.BlockSpec((TM, LANES), tab),
                  pl.BlockSpec((1, LANES), const2),
                  pl.BlockSpec((1, LANES), const2),
                  pl.BlockSpec((LANES, LANES), const2)],
        out_specs=[pl.BlockSpec((TM, RW_COLS), row),
                   pl.BlockSpec((None, 2, 4 * TM, LANES), lambda i: (i, 0, 0, 0)),
                   pl.BlockSpec((2, TM, LANES), lambda i: (0, i, 0)),
                   pl.BlockSpec((2, TM, LANES), lambda i: (0, i, 0)),
                   pl.BlockSpec((TM, 2 * D), row)],
        out_shape=[jax.ShapeDtypeStruct((n, RW_COLS), F32),
                   jax.ShapeDtypeStruct((n // TM, 2, 4 * TM, LANES), ATT_DT),
                   jax.ShapeDtypeStruct((2, n, LANES), ATT_DT),
                   jax.ShapeDtypeStruct((2, n, LANES), PV_DT),
                   jax.ShapeDtypeStruct((n, 2 * D), BF16)],
        compiler_params=_cp(("arbitrary",), 56),
        name="inproj",
    )(x2, sc1, sh1, g1, w_in_b, cos_t, sin_t, qg, kg, bd128)


def _prep_kernel(p_ref, pprev_ref, pnext_ref, mup_ref, mun_ref, wdec_ref, wicl_ref, gup_ref,
                 w0_ref, a0_ref, kk_ref, ka_ref, rk_ref, bd_ref,
                 r_o, v_o, kk_o, lwf_o, lwb_o, kf_o, kb_o, bf_o, bb_o, go_o, bv_o, *, tpb):
    i = pl.program_id(0)
    ib = i % tpb
    p = p_ref[...]
    row = lax.broadcasted_iota(jnp.int32, (TM, 1), 0)
    hp = jnp.where(ib == 0, 0.0, pprev_ref[7:8, :])
    hn = jnp.where(ib == tpb - 1, 0.0, pnext_ref[0:1, :])
    prev = jnp.where(row == 0, hp, pltpu.roll(p, 1, axis=0))
    nxt = jnp.where(row == TM - 1, hn, pltpu.roll(p, TM - 1, axis=0))
    ps = p + mup_ref[...] * (prev - p) + mun_ref[...] * (nxt - p)

    r = ps[:, 0:RW]
    k = ps[:, RW:2 * RW]
    v = ps[:, 2 * RW:3 * RW]
    dlo = ps[:, 3 * RW:3 * RW + LANES]
    alo = ps[:, 3 * RW + LANES:3 * RW + 2 * LANES]
    glo = ps[:, 3 * RW + 2 * LANES:3 * RW + 3 * LANES]
    bd = bd_ref[...]

    dec = _dot(jnp.tanh(dlo).astype(BF16), wdec_ref[...]) + w0_ref[...]
    icl = _dot(alo.astype(BF16), wicl_ref[...]) + a0_ref[...]
    lw = -DECAY_SCALE * jax.nn.sigmoid(dec)
    a = jax.nn.sigmoid(icl)
    g_out = _dot(jax.nn.sigmoid(glo).astype(BF16), gup_ref[...])

    kk = k * kk_ref[...]
    ssq = _group_sum((kk * kk).astype(BF16), bd)
    kk = kk * lax.rsqrt(jnp.maximum(ssq, 1e-24))
    ka = ka_ref[...]
    a_f = a[:, 0:RW]
    a_b = a[:, RW:2 * RW]
    k_f = k * (1.0 + (a_f - 1.0) * ka)
    k_b = k * (1.0 + (a_b - 1.0) * ka)
    bonus = _group_sum((r * (k_f + k_b) * rk_ref[...]).astype(BF16), bd)

    r_o[...] = r
    v_o[...] = v
    kk_o[...] = kk
    lwf_o[...] = lw[:, 0:RW]
    lwb_o[...] = lw[:, RW:2 * RW]
    kf_o[...] = k_f
    kb_o[...] = k_b
    bf_o[...] = a_f * kk
    bb_o[...] = a_b * kk
    go_o[...] = g_out
    bv_o[...] = bonus * v


def _prep(p_rw, mup, mun, wdec, wicl, gup, w0, a0, k_k, k_a, r_k, bd256, n_tok):
    n = p_rw.shape[0]
    tpb = n_tok // TM
    nt8 = n // 8
    row = lambda i: (i, 0)
    c2 = lambda i: (0, 0)
    out = jax.ShapeDtypeStruct((n, RW), F32)
    return pl.pallas_call(
        functools.partial(_prep_kernel, tpb=tpb),
        grid=(n // TM,),
        in_specs=[pl.BlockSpec((TM, RW_COLS), row),
                  pl.BlockSpec((8, RW_COLS), lambda i: (jnp.maximum(i * (TM // 8) - 1, 0), 0)),
                  pl.BlockSpec((8, RW_COLS), lambda i: (jnp.minimum((i + 1) * (TM // 8), nt8 - 1), 0)),
                  pl.BlockSpec((1, RW_COLS), c2), pl.BlockSpec((1, RW_COLS), c2),
                  pl.BlockSpec((LANES, 2 * RW), c2), pl.BlockSpec((LANES, 2 * RW), c2),
                  pl.BlockSpec((LANES, RW), c2),
                  pl.BlockSpec((1, 2 * RW), c2), pl.BlockSpec((1, 2 * RW), c2),
                  pl.BlockSpec((1, RW), c2), pl.BlockSpec((1, RW), c2), pl.BlockSpec((1, RW), c2),
                  pl.BlockSpec((HG, HG), c2)],
        out_specs=[pl.BlockSpec((TM, RW), row)] * 11,
        out_shape=[out] * 11,
        compiler_params=_cp(("arbitrary",)),
        name="rwkv_prep",
    )(p_rw, p_rw, p_rw, mup, mun, wdec, wicl, gup, w0, a0, k_k, k_a, r_k, bd256)


def _bd(x, bdmask):
    return jnp.where(bdmask, jnp.concatenate([x, x, x, x], axis=0), 0.0)


def _wkv_pre(sl, r_ref, v_ref, kk_ref, lw_ref, k_ref, b_ref, tri_ref, rev, cst):
    strict, incl, eye_w, bdmask = cst
    r = r_ref[sl, :]
    v = v_ref[sl, :]
    kk = kk_ref[sl, :]
    lw = lw_ref[sl, :]
    k = k_ref[sl, :]
    b = b_ref[sl, :]
    tri = tri_ref[...]
    l3 = _split3(lw)
    cs = _dot(tri, l3[0]) + _dot(tri, l3[1]) + _dot(tri, l3[2])
    yield
    tot = cs[0:1, :] if rev else cs[CHUNK - 1:CHUNK, :]
    e_neg = jnp.exp(-cs)
    e_end = jnp.exp(tot - cs)
    rq = r * jnp.exp(cs)
    kq = kk * jnp.exp(cs - lw)

    lhs1 = jnp.concatenate([kq, rq], axis=0).astype(BF16)
    rhs1 = jnp.concatenate([_bd(b * e_neg, bdmask), _bd(k * e_neg, bdmask)], axis=0).astype(BF16)
    a_all = _dot_nt(lhs1, rhs1)
    yield
    n_ub = jnp.where(strict, a_all[0:CHUNK, 0:HG], 0.0)
    a_uk = jnp.where(strict, a_all[0:CHUNK, HG:2 * HG], 0.0)
    a_rb = jnp.where(incl, a_all[CHUNK:2 * CHUNK, 0:HG], 0.0).astype(BF16)
    a_rk = jnp.where(incl, a_all[CHUNK:2 * CHUNK, HG:2 * HG], 0.0)

    n_sq = int(math.log2(CHUNK))
    m_j = -n_ub
    t_inv = jnp.where(eye_w, 1.0, 0.0) + m_j
    m_j = _dot(m_j.astype(BF16), _bd(m_j, bdmask).astype(BF16))
    yield
    for j in range(1, n_sq):
        mb = _bd(m_j, bdmask).astype(BF16)
        if j < n_sq - 1:
            both = _dot(jnp.concatenate([t_inv, m_j], axis=0).astype(BF16), mb)
            t_inv = t_inv + both[0:CHUNK]
            m_j = both[CHUNK:2 * CHUNK]
        else:
            t_inv = t_inv + _dot(t_inv.astype(BF16), mb)
        yield

    av = _dot(jnp.concatenate([a_uk, a_rk], axis=0).astype(BF16), _bd(v, bdmask).astype(BF16))
    yield
    tb = t_inv.astype(BF16)
    u0 = -_dot(tb, _bd(av[0:CHUNK], bdmask).astype(BF16))
    pm = _dot(tb, _bd(kq, bdmask).astype(BF16))
    yield
    lhs_t = jnp.concatenate([b * e_end, k * e_end], axis=0).T.astype(BF16)
    wcol = jnp.exp(jnp.broadcast_to(tot, (LANES, HG)).T)
    return dict(u0=u0, pmrq=jnp.concatenate([pm, rq], axis=0).astype(BF16), a_rb=a_rb, yv=av[CHUNK:2 * CHUNK],
                v=v.astype(BF16), lhs_t=lhs_t, wcol=jnp.concatenate([wcol, wcol], axis=1))


def _wkv_seq(sl, pre, st_ref, o_ref, bdmask):
    st = st_ref[...]
    g = _dot(pre["pmrq"], st.astype(BF16))
    yield
    u = pre["u0"] - g[0:CHUNK]
    upd = _dot(pre["lhs_t"], jnp.concatenate([u.astype(BF16), pre["v"]], axis=0))
    yield
    st_ref[...] = st * pre["wcol"] + jnp.where(bdmask, upd, 0.0)
    o_ref[sl, :] = g[CHUNK:2 * CHUNK] + pre["yv"] + _dot(pre["a_rb"], _bd(u, bdmask).astype(BF16))


def _lockstep(gens):
    results = [None] * len(gens)
    live = list(range(len(gens)))
    while live:
        for i in list(live):
            try:
                next(gens[i])
            except StopIteration as stop:
                results[i] = stop.value
                live.remove(i)
    return results


def _wkv_kernel(rf, vf, kkf, lwf, kf, bf, rb, vb, kkb, lwb, kb, bb, trif, trib, of, ob, stf, stb):
    @pl.when(pl.program_id(2) == 0)
    def _():
        stf[...] = jnp.zeros_like(stf)
        stb[...] = jnp.zeros_like(stb)

    row = lax.broadcasted_iota(jnp.int32, (CHUNK, HG), 0)
    s_idx = lax.broadcasted_iota(jnp.int32, (CHUNK, HG), 1) & (HD - 1)
    r2 = lax.broadcasted_iota(jnp.int32, (HG, HG), 0) >> 6
    c2 = lax.broadcasted_iota(jnp.int32, (HG, HG), 1) >> 6
    bdmask = r2 == c2
    eye_w = s_idx == row
    cst_f = (s_idx < row, s_idx <= row, eye_w, bdmask)
    cst_b = (s_idx > row, s_idx >= row, eye_w, bdmask)
    nc = WKV_BLK // CHUNK
    sls = [pl.ds(c * CHUNK, CHUNK) for c in range(nc)]
    pre = _lockstep([_wkv_pre(sl, rf, vf, kkf, lwf, kf, bf, trif, False, cst_f) for sl in sls]
                    + [_wkv_pre(sl, rb, vb, kkb, lwb, kb, bb, trib, True, cst_b) for sl in sls])
    for c in range(nc):
        _lockstep([_wkv_seq(sls[c], pre[c], stf, of, bdmask),
                   _wkv_seq(sls[nc - 1 - c], pre[nc + nc - 1 - c], stb, ob, bdmask)])


def _wkv(r, v, kk, lwf, lwb, kf, kb, bf, bb, trif, trib, bsz, n_tok):
    nb = n_tok // WKV_BLK
    sh3 = lambda a: a.reshape(bsz, n_tok, RW)
    fwd = pl.BlockSpec((None, WKV_BLK, HG), lambda b, g, i: (b, i, g))
    bwd = pl.BlockSpec((None, WKV_BLK, HG), lambda b, g, i: (b, nb - 1 - i, g))
    tri = pl.BlockSpec((CHUNK, CHUNK), lambda b, g, i: (0, 0))
    out = jax.ShapeDtypeStruct((bsz, n_tok, RW), F32)
    of, ob = pl.pallas_call(
        _wkv_kernel,
        grid=(bsz, RW // HG, nb),
        in_specs=[fwd] * 6 + [bwd] * 6 + [tri, tri],
        out_specs=[fwd, bwd],
        out_shape=[out, out],
        scratch_shapes=[pltpu.VMEM((HG, HG), F32), pltpu.VMEM((HG, HG), F32)],
        compiler_params=_cp(("arbitrary", "arbitrary", "arbitrary")),
        name="wkv",
    )(sh3(r), sh3(v), sh3(kk), sh3(lwf), sh3(kf), sh3(bf),
      sh3(r), sh3(v), sh3(kk), sh3(lwb), sh3(kb), sh3(bb), trif, trib)
    return of.reshape(-1, RW), ob.reshape(-1, RW)


def _attn_kernel(q_ref, k_ref, v_ref, o_ref, m_scr, acc_scr, *, n_kb, tk):
    q = q_ref[...]
    m_scr[...] = jnp.full_like(m_scr, NEG_BIG)
    acc_scr[...] = jnp.zeros_like(acc_scr)

    def body(kb, carry):
        sl = pl.ds(pl.multiple_of(kb * tk, tk), tk)
        s = _dot_nt(q, k_ref[sl, :])
        m_prev = m_scr[...]
        m_new = jnp.maximum(m_prev, jnp.max(s, axis=1, keepdims=True))
        alpha = jnp.exp2(m_prev - m_new)
        p = jnp.exp2(s - jnp.tile(m_new, (1, tk // LANES)))
        acc_scr[...] = alpha * acc_scr[...] + _dot(p.astype(PV_DT), v_ref[sl, :])
        m_scr[...] = m_new
        return carry

    lax.fori_loop(0, n_kb, body, 0, unroll=2 if n_kb % 2 == 0 else 1)
    acc = acc_scr[...]
    o = acc / acc[:, HD:HD + 1]
    low = lax.broadcasted_iota(jnp.int32, (TM, LANES), 1) < HD
    for pr in range(2):
        oa = o[(2 * pr) * TM:(2 * pr + 1) * TM, :]
        ob = o[(2 * pr + 1) * TM:(2 * pr + 2) * TM, :]
        o_ref[:, pr * LANES:(pr + 1) * LANES] = jnp.where(low, oa, pltpu.roll(ob, HD, axis=1)).astype(BF16)


def _attention(q_st, k, v, bsz, n_tok):
    nq = n_tok // TM
    tk = min(ATT_TK, n_tok)
    q5 = q_st.reshape(bsz, nq, 2, 4 * TM, LANES)
    k4 = k.reshape(2, bsz, n_tok, LANES)
    v4 = v.reshape(2, bsz, n_tok, LANES)
    kv = pl.BlockSpec((None, None, n_tok, LANES), lambda b, g, i: (g, b, 0, 0))
    o = pl.pallas_call(
        functools.partial(_attn_kernel, n_kb=n_tok // tk, tk=tk),
        grid=(bsz, 2, nq),
        in_specs=[pl.BlockSpec((None, None, None, 4 * TM, LANES), lambda b, g, i: (b, i, g, 0, 0)), kv, kv],
        out_specs=pl.BlockSpec((None, TM, 2 * LANES), lambda b, g, i: (b, i, g)),
        out_shape=jax.ShapeDtypeStruct((bsz, n_tok, RW), BF16),
        scratch_shapes=[pltpu.VMEM((4 * TM, LANES), F32)] * 2,
        compiler_params=_cp(("arbitrary", "arbitrary", "arbitrary"), 56),
        name="attention",
    )(q5, k4, v4)
    return o.reshape(-1, RW)


def _merge_kernel(of_ref, ob_ref, bv_ref, go_ref, oat_ref, gate_ref, x_ref, gt1_ref, sc2_ref, sh2_ref,
                  lnw_ref, lnb_ref, g2_ref, wbr_ref, wba_ref, wout_ref, wrh_ref, wrl_ref, br_ref,
                  bd_ref, tri_ref, x1_ref, h2_ref, route_ref, cnt_ref, carry):
    @pl.when(pl.program_id(0) == 0)
    def _():
        carry[...] = jnp.zeros_like(carry)

    bd = bd_ref[...]
    o = of_ref[...] + ob_ref[...]
    oh, om, _ = _split3(o)
    mu = (_group_sum(oh, bd) + _group_sum(om, bd)) * (1.0 / HD)
    d = o - mu
    dh, dm, _ = _split3(d * d)
    var = (_group_sum(dh, bd) + _group_sum(dm, bd)) * (1.0 / HD)
    on = d * lax.rsqrt(var + GN_EPS) * lnw_ref[...] + lnb_ref[...]
    o_rw = (on + bv_ref[...]) * go_ref[...]
    br = _dot(o_rw.astype(BF16), wbr_ref[...])
    ba = _dot(oat_ref[...], wba_ref[...])
    gates = gate_ref[...]
    merged = gates[:, 0:D].astype(F32) * br + gates[:, D:2 * D].astype(F32) * ba
    x1 = x_ref[...] + gt1_ref[...] * _dot(merged.astype(BF16), wout_ref[...])
    x1_ref[...] = x1
    ms = jnp.mean(x1 * x1, axis=-1, keepdims=True)
    h2 = x1 * lax.rsqrt(ms + EPS) * g2_ref[...]
    h2 = h2 * (1.0 + sc2_ref[...]) + sh2_ref[...]
    _store_rows(h2_ref, h2, MERGE_TM)

    hh, hm, hl = _split3(h2)
    wh = wrh_ref[...]
    wl = wrl_ref[...]
    logits = _dot(hh, wh) + (_dot(hh, wl) + _dot(hm, wh)) + (_dot(hm, wl) + _dot(hl, wh)) + br_ref[...]

    lane = lax.broadcasted_iota(jnp.int32, (MERGE_TM, LANES), 1).astype(F32)
    cur = logits
    vals, idxs = [], []
    for _ in range(TOP_K):
        m = jnp.max(cur, axis=1, keepdims=True)
        ix = jnp.min(jnp.where(cur == m, lane, float(LANES)), axis=1, keepdims=True)
        vals.append(m)
        idxs.append(ix)
        cur = jnp.where(lane == ix, -jnp.inf, cur)
    es = [jnp.exp(vv - vals[0]) for vv in vals]
    den = es[0] + es[1] + es[2] + es[3]
    onehot = jnp.zeros((MERGE_TM, LANES), F32)
    for ix in idxs:
        onehot = onehot + jnp.where(lane == ix, 1.0, 0.0)
    cnt = _dot(tri_ref[...], onehot.astype(BF16)) + carry[...]
    route = jnp.zeros((MERGE_TM, LANES), F32)
    for j in range(TOP_K):
        rank = jnp.sum(jnp.where(lane == idxs[j], cnt, 0.0), axis=1, keepdims=True)
        route = jnp.where(lane == float(j), idxs[j], route)
        route = jnp.where(lane == float(TOP_K + j), rank, route)
        route = jnp.where(lane == float(2 * TOP_K + j), es[j] / den, route)
    route_ref[...] = route
    carry[...] = carry[...] + jnp.sum(onehot, axis=0, keepdims=True)
    cnt_ref[...] = jnp.broadcast_to(carry[...], (8, LANES))


def _merge(of, ob, bv, go, oat, gates, x2, gt1, sc2, sh2, lnw, lnb, g2, wbr, wba, wout, wrh, wrl, brt,
           bd256, tri_tm, n_tok):
    n = x2.shape[0]
    tpb = n_tok // MERGE_TM
    row = lambda i: (i, 0)
    per_b = lambda i: (i // tpb, 0, 0)
    c2 = lambda i: (0, 0)
    rw = pl.BlockSpec((MERGE_TM, RW), row)
    full = lambda a: pl.BlockSpec(a.shape, c2)
    mod = pl.BlockSpec((None, 1, D), per_b)
    return pl.pallas_call(
        _merge_kernel,
        grid=(n // MERGE_TM,),
        in_specs=[rw, rw, rw, rw, pl.BlockSpec((MERGE_TM, RW), row), pl.BlockSpec((MERGE_TM, 2 * D), row),
                  pl.BlockSpec((MERGE_TM, D), row), mod, mod, mod,
                  full(lnw), full(lnb), full(g2), full(wbr), full(wba), full(wout), full(wrh), full(wrl),
                  full(brt), full(bd256), full(tri_tm)],
        out_specs=[pl.BlockSpec((MERGE_TM, D), row), pl.BlockSpec((MERGE_TM * ROW_SUB, LANES), row),
                   pl.BlockSpec((MERGE_TM, LANES), row),
                   pl.BlockSpec((8, LANES), c2)],
        out_shape=[jax.ShapeDtypeStruct((n, D), F32), jax.ShapeDtypeStruct((n * ROW_SUB, LANES), F32),
                   jax.ShapeDtypeStruct((n, LANES), F32), jax.ShapeDtypeStruct((8, LANES), F32)],
        scratch_shapes=[pltpu.VMEM((1, LANES), F32)],
        compiler_params=_cp(("arbitrary",), 56),
        name="merge",
    )(of, ob, bv, go, oat, gates, x2, gt1, sc2, sh2, lnw, lnb, g2, wbr, wba, wout, wrh, wrl, brt,
      bd256, tri_tm)


def _slots_kernel(route_ref, pstart_ref, o_ref):
    route = route_ref[...]
    ps = pstart_ref[...]
    lane = lax.broadcasted_iota(jnp.int32, route.shape, 1).astype(F32)
    out = jnp.zeros(route.shape, F32)
    for j in range(TOP_K):
        off = jnp.sum(jnp.where(lane == route[:, j:j + 1], ps, 0.0), axis=1, keepdims=True)
        out = jnp.where(lane == float(j), off + route[:, TOP_K + j:TOP_K + j + 1], out)
    o_ref[...] = out.astype(jnp.int32)


def _slots(route, pstart_row):
    n = route.shape[0]
    return pl.pallas_call(
        _slots_kernel,
        grid=(n // SLOT_TM,),
        in_specs=[pl.BlockSpec((SLOT_TM, LANES), lambda i: (i, 0)), pl.BlockSpec((1, LANES), lambda i: (0, 0))],
        out_specs=pl.BlockSpec((SLOT_TM, LANES), lambda i: (i, 0)),
        out_shape=jax.ShapeDtypeStruct((n, LANES), jnp.int32),
        compiler_params=_cp(("arbitrary",)),
        name="moe_slots",
    )(route, pstart_row)


def _dispatch_kernel(slot_ref, pend_ref, npad_ref, h2_ref, xs_hbm, zbuf, sem):
    @pl.when(pl.program_id(0) == 0)
    def _():
        zbuf[...] = jnp.zeros_like(zbuf)

        def fill_tail(blk, carry):
            first = pl.multiple_of(blk * (MOE_BM * ROW_SUB), MOE_BM * ROW_SUB)
            fill = pltpu.make_async_copy(zbuf, xs_hbm.at[pl.ds(first, MOE_BM * ROW_SUB)], sem)
            fill.start()
            fill.wait()
            return carry

        lax.fori_loop(pend_ref[N_EXPERTS - 1] // MOE_BM, xs_hbm.shape[0] // (MOE_BM * ROW_SUB), fill_tail, 0)
        for e in range(N_EXPERTS):
            @pl.when(npad_ref[e] > 0)
            def _():
                last = pl.multiple_of((pend_ref[e] - MOE_BM) * ROW_SUB, MOE_BM * ROW_SUB)
                fill = pltpu.make_async_copy(zbuf, xs_hbm.at[pl.ds(last, MOE_BM * ROW_SUB)], sem)
                fill.start()
                fill.wait()

    def body(t, carry):
        for j in range(TOP_K):
            pltpu.make_async_copy(h2_ref.at[_tile(t)], xs_hbm.at[_tile(slot_ref[t * TOP_K + j])],
                                  sem).start(priority=j % 2)
        return carry

    lax.fori_loop(0, TM, body, 0, unroll=DMA_UNROLL)
    for _ in range(TOP_K):
        pltpu.make_async_copy(h2_ref, xs_hbm.at[pl.ds(0, TM * ROW_SUB)], sem).wait()


def _dispatch(slot_flat, pad_end, padded, h2, cap):
    n = h2.shape[0] // ROW_SUB
    return pl.pallas_call(
        _dispatch_kernel,
        grid=(n // TM,),
        in_specs=[pl.BlockSpec((TM * TOP_K,), lambda i: (i,), memory_space=pltpu.SMEM),
                  pl.BlockSpec(memory_space=pltpu.SMEM),
                  pl.BlockSpec(memory_space=pltpu.SMEM),
                  pl.BlockSpec((TM * ROW_SUB, LANES), lambda i: (i, 0))],
        out_specs=pl.BlockSpec(memory_space=pl.ANY),
        out_shape=jax.ShapeDtypeStruct((cap * ROW_SUB, LANES), F32),
        scratch_shapes=[pltpu.VMEM((MOE_BM * ROW_SUB, LANES), F32), pltpu.SemaphoreType.DMA(())],
        compiler_params=_cp(("arbitrary",)),
        name="moe_dispatch",
    )(slot_flat, pad_end, padded, h2)


def _expert_kernel(bexp_ref, nused_ref, xs_ref, wgu_ref, bgu_ref, wd_ref, bdn_ref, ys_ref):
    i = pl.program_id(0)

    @pl.when(i < nused_ref[0])
    def _():
        gu = _dot(_load_rows(xs_ref, MOE_BM).astype(BF16), wgu_ref[...]) + bgu_ref[...]
        g_lin = jnp.minimum(gu[:, 0:D_FF], SWIGLU_LIMIT)
        u_lin = jnp.clip(gu[:, D_FF:2 * D_FF], -SWIGLU_LIMIT, SWIGLU_LIMIT)
        act = (u_lin + 1.0) * (g_lin * jax.nn.sigmoid(SWIGLU_ALPHA * g_lin))
        _store_rows(ys_ref, _dot(act.astype(BF16), wd_ref[...]) + bdn_ref[...], MOE_BM)

    @pl.when(i >= nused_ref[0])
    def _():
        ys_ref[...] = jnp.zeros_like(ys_ref)


def _experts(blk_exp, nused, xs, wgu_b, bgu, wd_b, bdn):
    cap = xs.shape[0] // ROW_SUB
    grid_spec = pltpu.PrefetchScalarGridSpec(
        num_scalar_prefetch=2,
        grid=(cap // MOE_BM,),
        in_specs=[pl.BlockSpec((MOE_BM * ROW_SUB, LANES), lambda i, be, nu: (jnp.minimum(i, nu[0] - 1), 0)),
                  pl.BlockSpec((None, D, 2 * D_FF), lambda i, be, nu: (be[i], 0, 0)),
                  pl.BlockSpec((None, 1, 2 * D_FF), lambda i, be, nu: (be[i], 0, 0)),
                  pl.BlockSpec((None, D_FF, D), lambda i, be, nu: (be[i], 0, 0)),
                  pl.BlockSpec((None, 1, D), lambda i, be, nu: (be[i], 0, 0))],
        out_specs=pl.BlockSpec((MOE_BM * ROW_SUB, LANES), lambda i, be, nu: (i, 0)),
    )
    return pl.pallas_call(
        _expert_kernel,
        grid_spec=grid_spec,
        out_shape=jax.ShapeDtypeStruct(xs.shape, F32),
        compiler_params=_cp(("arbitrary",), 56),
        name="moe_experts",
    )(blk_exp, nused, xs, wgu_b, bgu, wd_b, bdn)


def _combine_kernel(slot_ref, ys_hbm, route_ref, x1_ref, gt2_ref, gf_ref, o_ref, buf, sem):
    def body(t, carry):
        for j in range(TOP_K):
            pltpu.make_async_copy(ys_hbm.at[_tile(slot_ref[t * TOP_K + j])], buf.at[j, _tile(t)],
                                  sem).start(priority=j % 2)
        return carry

    lax.fori_loop(0, TM, body, 0, unroll=DMA_UNROLL)
    for j in range(TOP_K):
        pltpu.make_async_copy(ys_hbm.at[pl.ds(0, TM * ROW_SUB)], buf.at[j], sem).wait()
    route = route_ref[...]
    y = jnp.zeros((TM, D), F32)
    for j in range(TOP_K):
        y = y + route[:, 2 * TOP_K + j:2 * TOP_K + j + 1] * _load_rows(buf, TM, lead=(j,))
    x = x1_ref[...] + gt2_ref[...] * y
    ms = jnp.mean(x * x, axis=-1, keepdims=True)
    o_ref[...] = x * lax.rsqrt(ms + EPS) * gf_ref[...]


def _combine(slot_flat, ys, route, x1, gt2, gf, n_tok):
    n = x1.shape[0]
    tpb = n_tok // TM
    row = lambda i: (i, 0)
    return pl.pallas_call(
        _combine_kernel,
        grid=(n // TM,),
        in_specs=[pl.BlockSpec((TM * TOP_K,), lambda i: (i,), memory_space=pltpu.SMEM),
                  pl.BlockSpec(memory_space=pl.ANY),
                  pl.BlockSpec((TM, LANES), row),
                  pl.BlockSpec((TM, D), row),
                  pl.BlockSpec((None, 1, D), lambda i: (i // tpb, 0, 0)),
                  pl.BlockSpec((1, D), lambda i: (0, 0))],
        out_specs=pl.BlockSpec((TM, D), row),
        out_shape=jax.ShapeDtypeStruct((n, D), F32),
        scratch_shapes=[pltpu.VMEM((TOP_K, TM * ROW_SUB, LANES), F32), pltpu.SemaphoreType.DMA(())],
        compiler_params=_cp(("arbitrary",)),
        name="moe_combine",
    )(slot_flat, ys, route, x1, gt2, gf)


def _rope_tables(n_tok):
    t = jnp.arange(n_tok)
    row = (t // GRID_W).astype(F32)
    col = (t % GRID_W).astype(F32)
    nf = HD // 4
    freqs = ROPE_THETA ** (-jnp.arange(nf, dtype=F32) / nf)
    ang = jnp.concatenate([row[:, None] * freqs, col[:, None] * freqs], axis=-1)
    cos = jnp.repeat(jnp.cos(ang), 2, axis=-1)
    sin = jnp.repeat(jnp.sin(ang), 2, axis=-1)
    sign = jnp.tile(jnp.array([-1.0, 1.0], F32), HD // 2)
    return jnp.tile(cos, (1, 2)), jnp.tile(sin * sign, (1, 2))


def _block_diag2(a, b):
    z = jnp.zeros_like(a)
    return jnp.concatenate([jnp.concatenate([a, z], axis=1), jnp.concatenate([z, b], axis=1)], axis=0)


def _prepare_weights(w):
    f = {}
    f["w_in"] = w["w_in"][0].astype(BF16)
    f["g1"] = w["norm1_g"][0].reshape(1, D)
    f["g2"] = w["norm2_g"][0].reshape(1, D)
    f["qg"] = jnp.tile(w["q_norm_g"][0], 2).reshape(1, LANES)
    f["kg"] = jnp.tile(w["k_norm_g"][0], 2).reshape(1, LANES)
    ones = jnp.ones((HD, HD), F32)
    f["bd128"] = jnp.kron(jnp.eye(2, dtype=F32), ones).astype(BF16)
    f["bd256"] = jnp.kron(jnp.eye(HG // HD, dtype=F32), ones).astype(BF16)
    f["mup"] = w["mu_prev"][0].reshape(1, RW_COLS)
    f["mun"] = w["mu_next"][0].reshape(1, RW_COLS)
    f["wdec"] = _block_diag2(w["wb_f"][0], w["wb_b"][0]).astype(BF16)
    f["wicl"] = _block_diag2(w["ab_f"][0], w["ab_b"][0]).astype(BF16)
    f["gup"] = w["g_up"][0].astype(BF16)
    f["w0"] = jnp.concatenate([w["w0_f"][0], w["w0_b"][0]]).reshape(1, 2 * RW)
    f["a0"] = jnp.concatenate([w["a0_f"][0], w["a0_b"][0]]).reshape(1, 2 * RW)
    f["k_k"] = w["k_k"][0].reshape(1, RW)
    f["k_a"] = w["k_a"][0].reshape(1, RW)
    f["r_k"] = w["r_k"][0].reshape(1, RW)
    ti = jnp.arange(CHUNK)
    f["trif"] = (ti[None, :] <= ti[:, None]).astype(BF16)
    f["trib"] = (ti[None, :] >= ti[:, None]).astype(BF16)
    tm = jnp.arange(MERGE_TM)
    f["tri_tm"] = (tm[None, :] < tm[:, None]).astype(BF16)
    f["lnw"] = w["lnx_w"][0].reshape(1, RW)
    f["lnb"] = w["lnx_b"][0].reshape(1, RW)
    f["wbr"] = w["w_br_rwkv"][0].astype(BF16)
    f["wba"] = w["w_br_attn"][0].astype(BF16)
    f["wout"] = w["w_out"][0].astype(BF16)
    wr = jnp.pad(w["w_router"][0], ((0, 0), (0, LANES - N_EXPERTS)))
    wrh = wr.astype(BF16)
    f["wrh"] = wrh
    f["wrl"] = (wr - wrh.astype(F32)).astype(BF16)
    f["br"] = jnp.pad(w["b_router"][0], (0, LANES - N_EXPERTS), constant_values=NEG_BIG).reshape(1, LANES)
    f["wgu"] = w["w_gu"][0].astype(BF16)
    f["bgu"] = w["b_gu"][0].reshape(N_EXPERTS, 1, 2 * D_FF)
    f["wd"] = w["w_down"][0].astype(BF16)
    f["bdn"] = w["b_down"][0].reshape(N_EXPERTS, 1, D)
    f["gf"] = w["normf_g"].reshape(1, D)
    return f


def _run(x, c, w, f):
    bsz, n_tok, _ = x.shape
    n = bsz * n_tok
    x2 = x.reshape(n, D)
    c8 = jnp.pad(c, ((0, 8 - bsz), (0, 0)))
    mod = _ada(c8, w["w_ada"][0], w["b_ada"][0])[:bsz]
    sh1, sc1, gt1, sh2, sc2, gt2 = [m.reshape(bsz, 1, D) for m in jnp.split(mod, 6, axis=-1)]

    cos_t, sin_t = _rope_tables(n_tok)
    p_rw, q_st, k_att, v_att, gates = _inproj(x2, sc1, sh1, f["g1"], f["w_in"], cos_t, sin_t,
                                              f["qg"], f["kg"], f["bd128"], n_tok)
    r, v, kk, lwf, lwb, kf, kb, bf, bb, go, bv = _prep(
        p_rw, f["mup"], f["mun"], f["wdec"], f["wicl"], f["gup"], f["w0"], f["a0"],
        f["k_k"], f["k_a"], f["r_k"], f["bd256"], n_tok)
    of, ob = _wkv(r, v, kk, lwf, lwb, kf, kb, bf, bb, f["trif"], f["trib"], bsz, n_tok)
    o_at = _attention(q_st, k_att, v_att, bsz, n_tok)
    x1, h2, route, cnt = _merge(of, ob, bv, go, o_at, gates, x2, gt1, sc2, sh2, f["lnw"], f["lnb"], f["g2"],
                                f["wbr"], f["wba"], f["wout"], f["wrh"], f["wrl"], f["br"],
                                f["bd256"], f["tri_tm"], n_tok)

    counts = cnt[0, :N_EXPERTS].astype(jnp.int32)
    padded = ((counts + MOE_BM - 1) // MOE_BM * MOE_BM).astype(jnp.int32)
    pad_end = jnp.cumsum(padded).astype(jnp.int32)
    pstart_row = jnp.pad((pad_end - padded).astype(F32), (0, LANES - N_EXPERTS)).reshape(1, LANES)
    nk = n * TOP_K
    nblk = -(-nk // MOE_BM) + N_EXPERTS
    blk_first = (jnp.arange(nblk) * MOE_BM)[:, None]
    blk_exp = jnp.minimum(jnp.sum(pad_end[None, :] <= blk_first, axis=1), N_EXPERTS - 1).astype(jnp.int32)
    nused = (pad_end[-1:] // MOE_BM).astype(jnp.int32)

    slot_flat = _slots(route, pstart_row)[:, 0:TOP_K].reshape(nk)
    xs = _dispatch(slot_flat, pad_end, padded, h2, nblk * MOE_BM)
    ys = _experts(blk_exp, nused, xs, f["wgu"], f["bgu"], f["wd"], f["bdn"])
    y = _combine(slot_flat, ys, route, x1, gt2, f["gf"], n_tok)
    return y.reshape(bsz, n_tok, D)


def kernel(x_prompt, x_sample, c_prompt, c_sample, norm1_g, norm2_g, w_ada, b_ada, w_in, mu_prev, mu_next, w0_f, w0_b, wb_f, wb_b, a0_f, a0_b, ab_f, ab_b, k_k, k_a, r_k, g_up, lnx_w, lnx_b, q_norm_g, k_norm_g, w_br_rwkv, w_br_attn, w_out, w_router, b_router, w_gu, b_gu, w_down, b_down, normf_g):
    w = dict(norm1_g=norm1_g, norm2_g=norm2_g, w_ada=w_ada, b_ada=b_ada, w_in=w_in, mu_prev=mu_prev,
             mu_next=mu_next, w0_f=w0_f, w0_b=w0_b, wb_f=wb_f, wb_b=wb_b, a0_f=a0_f, a0_b=a0_b, ab_f=ab_f,
             ab_b=ab_b, k_k=k_k, k_a=k_a, r_k=r_k, g_up=g_up, lnx_w=lnx_w, lnx_b=lnx_b, q_norm_g=q_norm_g,
             k_norm_g=k_norm_g, w_br_rwkv=w_br_rwkv, w_br_attn=w_br_attn, w_out=w_out, w_router=w_router,
             b_router=b_router, w_gu=w_gu, b_gu=b_gu, w_down=w_down, b_down=b_down, normf_g=normf_g)
    f = _prepare_weights(w)
    return (_run(x_prompt, c_prompt, w, f), _run(x_sample, c_sample, w, f))
```
